```python
import jax
import jax.numpy as jnp
from jax import lax
import numpy as np

D_MODEL = 2048
BATCH = 2
SEQ = 16384
DEPTH = 2

MEM_LEN = 256
EPS = 1e-6
FOX_HEAD_DIM = 64
FOX_WIDTH = D_MODEL // 4
FOX_HEADS = FOX_WIDTH // FOX_HEAD_DIM
Q_BLOCK = 128
CONV_WIDTH = D_MODEL // 4
CONV_TAPS = 31
GLA_HEADS = 4
GLA_VAL_WIDTH = D_MODEL // 2
GLA_KEY_WIDTH = GLA_VAL_WIDTH // 2
GLA_DK = GLA_KEY_WIDTH // GLA_HEADS
GLA_DV = GLA_VAL_WIDTH // GLA_HEADS
GLA_GATE_RANK = 16
GLA_GATE_TAU = 16.0
GLA_CHUNK = 64
MIX_WIDTH = FOX_WIDTH + CONV_WIDTH + GLA_VAL_WIDTH
IN_SPLITS = (FOX_WIDTH, FOX_WIDTH, FOX_WIDTH, FOX_HEADS, CONV_WIDTH, CONV_WIDTH,
             GLA_KEY_WIDTH, GLA_KEY_WIDTH, GLA_VAL_WIDTH, GLA_GATE_RANK, GLA_VAL_WIDTH)
IN_WIDTH = sum(IN_SPLITS)
XA_HEADS = 4
XA_HEAD_DIM = 128
XA_WIDTH = XA_HEADS * XA_HEAD_DIM
D_FF = 5632
N_EXPERTS = 8
TOP_K = 2
D_FF_EXPERT = 7168
N_DENSE = (DEPTH + 1) // 2
N_MOE = DEPTH // 2

kernel_name = 'hybrid_fox_conformer_gla_moe'


def rms_norm(x, g):
    xf = x.astype(jnp.float32)
    y = xf * lax.rsqrt(jnp.mean(xf * xf, axis=-1, keepdims=True) + EPS)
    return (y * g.astype(jnp.float32)).astype(x.dtype)


def swiglu(h, wg, wu, wd):
    return (jax.nn.silu(h @ wg) * (h @ wu)) @ wd


def split_points():
    pts, acc = [], 0
    for w in IN_SPLITS[:-1]:
        acc += w
        pts.append(acc)
    return pts


def forgetting_attention(q, k, v, log_f):
    B, S, H, Dh = q.shape
    nb = S // Q_BLOCK
    cum = jnp.cumsum(log_f, axis=1)
    cum_h = cum.transpose(0, 2, 1)
    qb = q.reshape(B, nb, Q_BLOCK, H, Dh).transpose(1, 0, 2, 3, 4)
    cb = cum.reshape(B, nb, Q_BLOCK, H).transpose(1, 0, 3, 2)
    kpos = jnp.arange(S)
    scale = Dh ** -0.5

    def block(args):
        i, q_i, c_i = args
        s = jnp.einsum('bqhd,bkhd->bhqk', q_i, k, preferred_element_type=jnp.float32) * scale
        s = s + (c_i[..., :, None] - cum_h[:, :, None, :])
        qpos = i * Q_BLOCK + jnp.arange(Q_BLOCK)
        s = jnp.where(kpos[None, :] <= qpos[:, None], s, -jnp.inf)
        p = jax.nn.softmax(s, axis=-1)
        return jnp.einsum('bhqk,bkhd->bqhd', p.astype(v.dtype), v)

    out = lax.map(block, (jnp.arange(nb), qb, cb))
    return out.transpose(1, 0, 2, 3, 4).reshape(B, S, H, Dh)


def conformer_conv(a, gate, w, b, ln_g, ln_b):
    u = a * jax.nn.sigmoid(gate)
    u = jnp.pad(u, ((0, 0), (CONV_TAPS - 1, 0), (0, 0)))
    y = lax.conv_general_dilated(u, w[:, None, :].astype(u.dtype), window_strides=(1,),
                                 padding='VALID', dimension_numbers=('NWC', 'WIO', 'NWC'),
                                 feature_group_count=CONV_WIDTH)
    y = (y + b).astype(jnp.float32)
    mu = jnp.mean(y, axis=-1, keepdims=True)
    var = jnp.mean(jnp.square(y - mu), axis=-1, keepdims=True)
    y = (y - mu) * lax.rsqrt(var + EPS) * ln_g.astype(jnp.float32) + ln_b.astype(jnp.float32)
    return jax.nn.silu(y).astype(a.dtype)


def gla_chunked(q, k, v, log_a):
    B, S, H, Dk = q.shape
    Dv = v.shape[-1]
    C = GLA_CHUNK
    N = S // C
    f32 = jnp.float32

    def chunks(t):
        return t.astype(f32).reshape(B, N, C, H, t.shape[-1]).transpose(1, 0, 3, 2, 4)

    qc = chunks(q) * (Dk ** -0.5)
    kc, vc, ac = chunks(k), chunks(v), chunks(log_a)
    bcum = jnp.cumsum(ac, axis=3)
    b_ref = bcum[:, :, :, C // 2:C // 2 + 1, :]
    q_rel = qc * jnp.exp(bcum - b_ref)
    k_rel = kc * jnp.exp(b_ref - bcum)
    scores = jnp.einsum('nbhtd,nbhsd->nbhts', q_rel, k_rel)
    causal = jnp.tril(jnp.ones((C, C), dtype=bool))
    o_intra = jnp.einsum('nbhts,nbhsv->nbhtv', jnp.where(causal, scores, 0.0), vc)
    b_last = bcum[:, :, :, -1, :]
    q_in = qc * jnp.exp(bcum)
    k_out = kc * jnp.exp(b_last[:, :, :, None, :] - bcum)

    def step(state, inp):
        q_n, k_n, v_n, decay_n = inp
        o_n = jnp.einsum('bhtd,bhdv->bhtv', q_n, state)
        state = jnp.exp(decay_n)[..., None] * state + jnp.einsum('bhsd,bhsv->bhdv', k_n, v_n)
        return state, o_n

    state0 = jnp.zeros((B, H, Dk, Dv), f32)
    _, o_inter = lax.scan(step, state0, (q_in, k_out, vc, b_last))
    o = (o_intra + o_inter).transpose(1, 0, 3, 2, 4).reshape(B, S, H, Dv)
    return o.astype(v.dtype)


def hybrid_mixer(hn, w_in, b_f, q_g, k_g, conv_w, conv_b, ln_g, ln_b, gw2, gb, gla_g, w_out):
    B, S, _ = hn.shape
    proj = hn @ w_in
    fq, fk, fv, ff, ca, cg, gq, gk, gv, glr, gr = jnp.split(proj, split_points(), axis=-1)
    fq = rms_norm(fq.reshape(B, S, FOX_HEADS, FOX_HEAD_DIM), q_g)
    fk = rms_norm(fk.reshape(B, S, FOX_HEADS, FOX_HEAD_DIM), k_g)
    fv = fv.reshape(B, S, FOX_HEADS, FOX_HEAD_DIM)
    log_f = jax.nn.log_sigmoid((ff + b_f).astype(jnp.float32))
    fox = forgetting_attention(fq, fk, fv, log_f).reshape(B, S, FOX_WIDTH)
    conv = conformer_conv(ca, cg, conv_w, conv_b, ln_g, ln_b)
    log_a = jax.nn.log_sigmoid((glr @ gw2 + gb).astype(jnp.float32)) / GLA_GATE_TAU
    o = gla_chunked(gq.reshape(B, S, GLA_HEADS, GLA_DK), gk.reshape(B, S, GLA_HEADS, GLA_DK),
                    gv.reshape(B, S, GLA_HEADS, GLA_DV), log_a.reshape(B, S, GLA_HEADS, GLA_DK))
    o = rms_norm(o, gla_g) * jax.nn.silu(gr.reshape(B, S, GLA_HEADS, GLA_DV))
    gla = o.reshape(B, S, GLA_VAL_WIDTH)
    return jnp.concatenate([fox, conv, gla], axis=-1) @ w_out


def memory_cross_attention(hn, mem_n, wq, wk, wv, wo, q_g, k_g):
    B, S, _ = hn.shape
    M = mem_n.shape[1]
    q = rms_norm((hn @ wq).reshape(B, S, XA_HEADS, XA_HEAD_DIM), q_g)
    k = rms_norm((mem_n @ wk).reshape(B, M, XA_HEADS, XA_HEAD_DIM), k_g)
    v = (mem_n @ wv).reshape(B, M, XA_HEADS, XA_HEAD_DIM)
    s = jnp.einsum('bshd,bmhd->bhsm', q, k, preferred_element_type=jnp.float32) * (XA_HEAD_DIM ** -0.5)
    p = jax.nn.softmax(s, axis=-1)
    o = jnp.einsum('bhsm,bmhd->bshd', p.astype(v.dtype), v).reshape(B, S, XA_WIDTH)
    return o @ wo


def moe_swiglu(hn, rw, rb, wg, wu, wd):
    logits = (hn @ rw).astype(jnp.float32) + rb.astype(jnp.float32)
    top_v, top_i = lax.top_k(logits, TOP_K)
    top_w = jax.nn.softmax(top_v, axis=-1)
    gates = jnp.sum(jax.nn.one_hot(top_i, N_EXPERTS, dtype=jnp.float32) * top_w[..., None], axis=-2)
    out = jnp.zeros_like(hn)
    for e in range(N_EXPERTS):
        out = out + gates[..., e:e + 1].astype(hn.dtype) * swiglu(hn, wg[e], wu[e], wd[e])
    return out


def setup_inputs(seed: int = 0) -> dict:
    key = jax.random.key(seed)
    ks = iter(jax.random.split(key, 40))
    f32 = jnp.float32
    L = DEPTH

    def w(shape, fan_in):
        return jax.random.normal(next(ks), shape, f32) * (fan_in ** -0.5)

    def gain(shape):
        return 1.0 + 0.02 * jax.random.normal(next(ks), shape, f32)

    def bias(shape, s=0.02):
        return s * jax.random.normal(next(ks), shape, f32)

    return {
        'x': jax.random.normal(next(ks), (BATCH, SEQ, D_MODEL), f32),
        'mem': jax.random.normal(next(ks), (BATCH, MEM_LEN, D_MODEL), f32),
        'mem_norm': gain((D_MODEL,)),
        'mix_norm': gain((L, D_MODEL)),
        'w_in': w((L, D_MODEL, IN_WIDTH), D_MODEL),
        'fox_forget_bias': jax.random.uniform(next(ks), (L, FOX_HEADS), f32, 1.0, 4.0),
        'fox_q_norm': gain((L, FOX_HEAD_DIM)),
        'fox_k_norm': gain((L, FOX_HEAD_DIM)),
        'conv_w': w((L, CONV_TAPS, CONV_WIDTH), CONV_TAPS),
        'conv_b': bias((L, CONV_WIDTH)),
        'conv_ln_g': gain((L, CONV_WIDTH)),
        'conv_ln_b': bias((L, CONV_WIDTH)),
        'gla_gate_w2': w((L, GLA_GATE_RANK, GLA_KEY_WIDTH), GLA_GATE_RANK),
        'gla_gate_b': bias((L, GLA_KEY_WIDTH)),
        'gla_out_norm': gain((L, GLA_DV)),
        'w_out': w((L, MIX_WIDTH, D_MODEL), MIX_WIDTH),
        'xa_norm': gain((L, D_MODEL)),
        'xa_wq': w((L, D_MODEL, XA_WIDTH), D_MODEL),
        'xa_wk': w((L, D_MODEL, XA_WIDTH), D_MODEL),
        'xa_wv': w((L, D_MODEL, XA_WIDTH), D_MODEL),
        'xa_q_norm': gain((L, XA_HEAD_DIM)),
        'xa_k_norm': gain((L, XA_HEAD_DIM)),
        'xa_wo': w((L, XA_WIDTH, D_MODEL), XA_WIDTH),
        'ffn_norm': gain((L, D_MODEL)),
        'dense_wg': w((N_DENSE, D_MODEL, D_FF), D_MODEL),
        'dense_wu': w((N_DENSE, D_MODEL, D_FF), D_MODEL),
        'dense_wd': w((N_DENSE, D_FF, D_MODEL), D_FF),
        'router_w': w((N_MOE, D_MODEL, N_EXPERTS), D_MODEL),
        'router_b': bias((N_MOE, N_EXPERTS), 0.01),
        'moe_wg': w((N_MOE, N_EXPERTS, D_MODEL, D_FF_EXPERT), D_MODEL),
        'moe_wu': w((N_MOE, N_EXPERTS, D_MODEL, D_FF_EXPERT), D_MODEL),
        'moe_wd': w((N_MOE, N_EXPERTS, D_FF_EXPERT, D_MODEL), D_FF_EXPERT),
    }


def reference(x, mem, mem_norm, mix_norm, w_in, fox_forget_bias, fox_q_norm, fox_k_norm,
              conv_w, conv_b, conv_ln_g, conv_ln_b, gla_gate_w2, gla_gate_b, gla_out_norm,
              w_out, xa_norm, xa_wq, xa_wk, xa_wv, xa_q_norm, xa_k_norm, xa_wo, ffn_norm,
              dense_wg, dense_wu, dense_wd, router_w, router_b, moe_wg, moe_wu, moe_wd):
    mem_n = rms_norm(mem, mem_norm)
    h = x
    for l in range(DEPTH):
        h = h + hybrid_mixer(rms_norm(h, mix_norm[l]), w_in[l], fox_forget_bias[l],
                             fox_q_norm[l], fox_k_norm[l], conv_w[l], conv_b[l],
                             conv_ln_g[l], conv_ln_b[l], gla_gate_w2[l], gla_gate_b[l],
                             gla_out_norm[l], w_out[l])
        h = h + memory_cross_attention(rms_norm(h, xa_norm[l]), mem_n, xa_wq[l], xa_wk[l],
                                       xa_wv[l], xa_wo[l], xa_q_norm[l], xa_k_norm[l])
        hn = rms_norm(h, ffn_norm[l])
        j = l // 2
        if l % 2 == 0:
            h = h + swiglu(hn, dense_wg[j], dense_wu[j], dense_wd[j])
        else:
            h = h + moe_swiglu(hn, router_w[j], router_b[j], moe_wg[j], moe_wu[j], moe_wd[j])
    return h
```

```python
import functools

import jax
import jax.numpy as jnp
from jax import lax
from jax.experimental import pallas as pl
from jax.experimental.pallas import tpu as pltpu

F32 = jnp.float32
BF16 = jnp.bfloat16
EPS = 1e-6

FOX_HEADS = 8
FOX_HEAD_DIM = 64
CONV_TAPS = 31
CONV_HALO = 32
GLA_HEADS = 4
GLA_DK = 128
GLA_DV = 256
GLA_GATE_RANK = 16
GLA_GATE_TAU = 16.0
GLA_CHUNK = 64
XA_HEADS = 4
XA_HEAD_DIM = 128
N_EXPERTS = 8
LANES = 128

FOX_W = FOX_HEADS * FOX_HEAD_DIM
CONV_W = 512
GLA_KW = GLA_HEADS * GLA_DK
GLA_VW = GLA_HEADS * GLA_DV
COL_FQ, COL_FK, COL_FV = 0, 512, 1024
COL_CA, COL_CG = 1536, 2048
COL_GQ, COL_GK, COL_GV, COL_GR = 2560, 3072, 3584, 4608
COL_SMALL = 5632
PACK_W = COL_SMALL + LANES

FOX_SKIP_MARGIN = 110.0
NEG_BIG = -1e30

VMEM_LIMIT = 56 * 1024 * 1024


def _params(*sem):
    return pltpu.CompilerParams(dimension_semantics=sem, vmem_limit_bytes=VMEM_LIMIT)


def _dot(a, b):
    return jnp.dot(a, b, preferred_element_type=F32)


def _dot_nt(a, b):
    return lax.dot_general(a, b, (((1,), (1,)), ((), ())), preferred_element_type=F32)


def _split3(x):
    hi = x.astype(BF16)
    r = x - hi.astype(F32)
    mid = r.astype(BF16)
    lo = (r - mid.astype(F32)).astype(BF16)
    return hi, mid, lo


def _log_sigmoid(z):
    return jnp.minimum(z, 0.0) - jnp.log1p(jnp.exp(-jnp.abs(z)))


def _silu(z):
    return z * (1.0 / (1.0 + jnp.exp(-z)))


def _rms(x, g):
    ms = jnp.mean(x * x, axis=-1, keepdims=True)
    return x * lax.rsqrt(ms + EPS) * g


def _norm_mm_kernel(h_ref, g_ref, w_ref, o_ref, hn_ref):
    @pl.when(pl.program_id(1) == 0)
    def _():
        hn_ref[...] = _rms(h_ref[...], g_ref[...]).astype(BF16)

    o_ref[...] = _dot(hn_ref[...], w_ref[...]).astype(o_ref.dtype)


def norm_matmul(h, g, w, tm, tn):
    t, d = h.shape
    n = w.shape[1]
    return pl.pallas_call(
        _norm_mm_kernel,
        grid=(t // tm, n // tn),
        in_specs=[pl.BlockSpec((tm, d), lambda i, j: (i, 0)),
                  pl.BlockSpec((1, d), lambda i, j: (0, 0)),
                  pl.BlockSpec((d, tn), lambda i, j: (0, j))],
        out_specs=pl.BlockSpec((tm, tn), lambda i, j: (i, j)),
        out_shape=jax.ShapeDtypeStruct((t, n), BF16),
        scratch_shapes=[pltpu.VMEM((tm, d), BF16)],
        compiler_params=_params("parallel", "arbitrary"),
    )(h, g, w)


def _fox_prep_kernel(q_ref, k_ref, v_ref, sm_ref, bf_ref, qg_ref, kg_ref, tri_ref,
                     qa_ref, ka_ref, va_ref, cf_ref, cl_ref, carry_ref):
    ts = q_ref.shape[0]

    @pl.when(pl.program_id(1) == 0)
    def _():
        carry_ref[...] = jnp.zeros_like(carry_ref)

    lf = _log_sigmoid(sm_ref[...].astype(F32) + bf_ref[...])
    hi, mid, lo = _split3(lf)
    tri = tri_ref[...]
    cum = _dot(tri, hi) + _dot(tri, mid) + _dot(tri, lo) + carry_ref[...]
    carry_ref[...] = cum[ts - 1:ts, :]
    cf_ref[...] = cum[0:1, :]
    cl_ref[...] = cum[ts - 1:ts, :]

    lane = lax.broadcasted_iota(jnp.int32, (ts, FOX_HEAD_DIM), 1)
    scale = FOX_HEAD_DIM ** -0.5
    for h in range(FOX_HEADS):
        sl = slice(h * FOX_HEAD_DIM, (h + 1) * FOX_HEAD_DIM)
        c = cum[:, h:h + 1]
        c_hi = c.astype(BF16).astype(F32)
        r = c - c_hi
        c_mid = r.astype(BF16).astype(F32)
        c_lo = r - c_mid
        qn = _rms(q_ref[:, sl].astype(F32), qg_ref[...]) * scale
        kn = _rms(k_ref[:, sl].astype(F32), kg_ref[...])
        q_extra = jnp.where(lane == 0, c_hi, jnp.where(lane == 1, c_mid, jnp.where(
            lane == 2, c_lo, jnp.where(lane < 6, 1.0, 0.0))))
        k_extra = jnp.where(lane < 3, 1.0, jnp.where(lane == 3, -c_hi, jnp.where(
            lane == 4, -c_mid, jnp.where(lane == 5, -c_lo, 0.0))))
        v_extra = jnp.where(lane == 0, 1.0, 0.0)
        qa_ref[h] = jnp.concatenate([qn, q_extra], axis=-1).astype(BF16)
        ka_ref[h] = jnp.concatenate([kn, k_extra], axis=-1).astype(BF16)
        va_ref[h] = jnp.concatenate([v_ref[:, sl].astype(F32), v_extra], axis=-1).astype(BF16)


def fox_prep(proj, bf_pad, qg, kg, ts):
    b, s, _ = proj.shape
    nt = s // ts
    tri = jnp.tril(jnp.ones((ts, ts), F32)).astype(BF16)
    aug = jax.ShapeDtypeStruct((b, FOX_HEADS, s, LANES), BF16)
    edge = jax.ShapeDtypeStruct((b, nt, 1, LANES), F32)
    col = lambda c, w: pl.BlockSpec((None, ts, w), lambda bi, i: (bi, i, c // w))
    aug_spec = pl.BlockSpec((None, FOX_HEADS, ts, LANES), lambda bi, i: (bi, 0, i, 0))
    edge_spec = pl.BlockSpec((None, None, 1, LANES), lambda bi, i: (bi, i, 0, 0))
    const = lambda shape: pl.BlockSpec(shape, lambda bi, i: (0,) * len(shape))
    return pl.pallas_call(
        _fox_prep_kernel,
        grid=(b, nt),
        in_specs=[col(COL_FQ, FOX_W), col(COL_FK, FOX_W), col(COL_FV, FOX_W), col(COL_SMALL, LANES),
                  const((1, LANES)), const((1, FOX_HEAD_DIM)), const((1, FOX_HEAD_DIM)), const((ts, ts))],
        out_specs=[aug_spec, aug_spec, aug_spec, edge_spec, edge_spec],
        out_shape=[aug, aug, aug, edge, edge],
        scratch_shapes=[pltpu.VMEM((1, LANES), F32)],
        compiler_params=_params("parallel", "arbitrary"),
    )(proj, proj, proj, proj, bf_pad, qg, kg, tri)


def _fox_flash_kernel(jlo_ref, q_ref, k_ref, v_ref, o_ref):
    tq = q_ref.shape[0]
    nq = pl.num_programs(2)
    i = pl.program_id(2)
    flat = (pl.program_id(0) * pl.num_programs(1) + pl.program_id(1)) * nq + i
    q = q_ref[...]

    def step(j, carry, masked):
        m, acc = carry
        rows = pl.ds(pl.multiple_of(j * tq, tq), tq)
        s = _dot_nt(q, k_ref[rows, :])
        if masked:
            qpos = lax.broadcasted_iota(jnp.int32, (tq, tq), 0)
            kpos = lax.broadcasted_iota(jnp.int32, (tq, tq), 1)
            s = jnp.where(kpos <= qpos, s, NEG_BIG)
        m_new = jnp.maximum(m, jnp.max(s, axis=-1, keepdims=True))
        p = jnp.exp(s - m_new).astype(BF16)
        acc = jnp.exp(m - m_new) * acc + _dot(p, v_ref[rows, :])
        return m_new, acc

    init = (jnp.full((tq, 1), NEG_BIG, F32), jnp.zeros((tq, LANES), F32))
    carry = lax.fori_loop(jlo_ref[flat], i, lambda j, c: step(j, c, False), init)
    _, acc = step(i, carry, True)
    o_ref[...] = (acc[:, :FOX_HEAD_DIM] / acc[:, FOX_HEAD_DIM:FOX_HEAD_DIM + 1]).astype(o_ref.dtype)


def fox_flash(q_aug, k_aug, v_aug, jlo, tq):
    b, nh, s, _ = q_aug.shape
    nq = s // tq
    full = pl.BlockSpec((None, None, s, LANES), lambda bi, h, i, jl: (bi, h, 0, 0))
    return pl.pallas_call(
        _fox_flash_kernel,
        grid_spec=pltpu.PrefetchScalarGridSpec(
            num_scalar_prefetch=1,
            grid=(b, nh, nq),
            in_specs=[pl.BlockSpec((None, None, tq, LANES), lambda bi, h, i, jl: (bi, h, i, 0)), full, full],
            out_specs=pl.BlockSpec((None, None, tq, FOX_HEAD_DIM), lambda bi, h, i, jl: (bi, h, i, 0)),
        ),
        out_shape=jax.ShapeDtypeStruct((b, nh, s, FOX_HEAD_DIM), BF16),
        compiler_params=_params("parallel", "parallel", "arbitrary"),
    )(jlo, q_aug, k_aug, v_aug)


def fox_first_block(c_first, c_last, qg, kg):
    cf = c_first[:, :, 0, :FOX_HEADS].transpose(0, 2, 1)
    cl = c_last[:, :, 0, :FOX_HEADS].transpose(0, 2, 1)
    nq = cf.shape[-1]
    qk_bound = 2.0 * FOX_HEAD_DIM ** 0.5 * jnp.max(jnp.abs(qg)) * jnp.max(jnp.abs(kg)) * (1.0 + 1e-3)
    gap = cf[..., :, None] - cl[..., None, :]
    j_idx = jnp.arange(nq)
    skippable = (qk_bound + gap < -FOX_SKIP_MARGIN) & (j_idx[None, :] < j_idx[:, None])
    return jnp.sum(skippable, axis=-1).astype(jnp.int32).reshape(-1)


def _conv_kernel(a_ref, g_ref, ap_ref, gp_ref, w_ref, b_ref, lg_ref, lb_ref, o_ref, u_ref):
    ts = a_ref.shape[0]

    def glu(a, g):
        g = g.astype(F32)
        return a.astype(F32) * (1.0 / (1.0 + jnp.exp(-g)))

    prev = glu(ap_ref[...], gp_ref[...])
    u_ref[0:CONV_HALO, :] = jnp.where(pl.program_id(1) > 0, prev, 0.0)
    u_ref[CONV_HALO:, :] = glu(a_ref[...], g_ref[...])
    first = CONV_HALO - (CONV_TAPS - 1)
    y = jnp.zeros((ts, CONV_W), F32) + b_ref[...]
    for j in range(CONV_TAPS):
        y = y + w_ref[j:j + 1, :] * u_ref[first + j:first + j + ts, :]
    mu = jnp.mean(y, axis=-1, keepdims=True)
    yc = y - mu
    var = jnp.mean(yc * yc, axis=-1, keepdims=True)
    y = yc * lax.rsqrt(var + EPS) * lg_ref[...] + lb_ref[...]
    o_ref[...] = _silu(y).astype(o_ref.dtype)


def conformer_conv(proj, w_pad, b, ln_g, ln_b, ts):
    bsz, s, _ = proj.shape
    ratio = ts // CONV_HALO
    main = lambda c: pl.BlockSpec((None, ts, CONV_W), lambda bi, i: (bi, i, c // CONV_W))
    halo = lambda c: pl.BlockSpec((None, CONV_HALO, CONV_W),
                                  lambda bi, i: (bi, jnp.maximum(i * ratio - 1, 0), c // CONV_W))
    const = lambda shape: pl.BlockSpec(shape, lambda bi, i: (0, 0))
    return pl.pallas_call(
        _conv_kernel,
        grid=(bsz, s // ts),
        in_specs=[main(COL_CA), main(COL_CG), halo(COL_CA), halo(COL_CG),
                  const((CONV_HALO, CONV_W)), const((1, CONV_W)), const((1, CONV_W)), const((1, CONV_W))],
        out_specs=pl.BlockSpec((None, ts, CONV_W), lambda bi, i: (bi, i, 0)),
        out_shape=jax.ShapeDtypeStruct((bsz, s, CONV_W), BF16),
        scratch_shapes=[pltpu.VMEM((ts + CONV_HALO, CONV_W), F32)],
        compiler_params=_params("parallel", "parallel"),
    )(proj, proj, proj, proj, w_pad, b, ln_g, ln_b)


def _gla_kernel(q_ref, k_ref, v0_ref, v1_ref, v2_ref, v3_ref, r0_ref, r1_ref, r2_ref, r3_ref,
                sm_ref, w2_ref, gb_ref, og_ref, tri_ref, o_ref, state_ref, bcum_ref):
    ts = q_ref.shape[0]
    c_len = GLA_CHUNK
    v_refs = (v0_ref, v1_ref, v2_ref, v3_ref)
    r_refs = (r0_ref, r1_ref, r2_ref, r3_ref)

    @pl.when(pl.program_id(1) == 0)
    def _():
        state_ref[...] = jnp.zeros_like(state_ref)

    log_a = _log_sigmoid(_dot(sm_ref[...], w2_ref[...]) + gb_ref[...]) * (1.0 / GLA_GATE_TAU)
    hi, mid, lo = _split3(log_a)
    tri = tri_ref[...]
    bcum_ref[...] = _dot(tri, hi) + _dot(tri, mid) + _dot(tri, lo)

    rr = lax.broadcasted_iota(jnp.int32, (c_len, c_len), 0)
    cc = lax.broadcasted_iota(jnp.int32, (c_len, c_len), 1)
    causal = cc <= rr
    scale = GLA_DK ** -0.5

    def chunk(ci, _):
        rows = pl.ds(pl.multiple_of(ci * c_len, c_len), c_len)
        for h in range(GLA_HEADS):
            hs = slice(h * GLA_DK, (h + 1) * GLA_DK)
            bc = bcum_ref[rows, hs]
            b_mid = bc[c_len // 2:c_len // 2 + 1, :]
            b_last = bc[c_len - 1:c_len, :]
            q = q_ref[rows, hs].astype(F32) * scale
            k = k_ref[rows, hs].astype(F32)
            v = v_refs[h][rows, :]
            q_rel = (q * jnp.exp(bc - b_mid)).astype(BF16)
            k_rel = (k * jnp.exp(b_mid - bc)).astype(BF16)
            scores = jnp.where(causal, _dot_nt(q_rel, k_rel), 0.0)
            state = state_ref[h]
            o = _dot(scores.astype(BF16), v) + _dot((q * jnp.exp(bc)).astype(BF16), state.astype(BF16))
            k_out_t = jnp.transpose(k * jnp.exp(b_last - bc))
            decay = jnp.transpose(jnp.broadcast_to(jnp.exp(b_last), (8, GLA_DK)))[:, 0:1]
            state_ref[h] = decay * state + _dot(k_out_t.astype(BF16), v)
            on = _rms(o, og_ref[...])
            o_ref[rows, h * GLA_DV:(h + 1) * GLA_DV] = (on * _silu(r_refs[h][rows, :].astype(F32))).astype(o_ref.dtype)
        return 0

    lax.fori_loop(0, ts // c_len, chunk, 0)


def gla(proj, w2_pad, gb, og, ts):
    bsz, s, _ = proj.shape
    blk = jnp.arange(ts) // GLA_CHUNK
    tri = (jnp.tril(jnp.ones((ts, ts), F32)) * (blk[:, None] == blk[None, :])).astype(BF16)
    col = lambda c, w: pl.BlockSpec((None, ts, w), lambda bi, i: (bi, i, c // w))
    const = lambda shape: pl.BlockSpec(shape, lambda bi, i: (0, 0))
    v_specs = [col(COL_GV + h * GLA_DV, GLA_DV) for h in range(GLA_HEADS)]
    r_specs = [col(COL_GR + h * GLA_DV, GLA_DV) for h in range(GLA_HEADS)]
    return pl.pallas_call(
        _gla_kernel,
        grid=(bsz, s // ts),
        in_specs=[col(COL_GQ, GLA_KW), col(COL_GK, GLA_KW)] + v_specs + r_specs + [
            col(COL_SMALL, LANES), const((LANES, GLA_KW)), const((1, GLA_KW)), const((1, GLA_DV)), const((ts, ts))],
        out_specs=pl.BlockSpec((None, ts, GLA_VW), lambda bi, i: (bi, i, 0)),
        out_shape=jax.ShapeDtypeStruct((bsz, s, GLA_VW), BF16),
        scratch_shapes=[pltpu.VMEM((GLA_HEADS, GLA_DK, GLA_DV), F32), pltpu.VMEM((ts, GLA_KW), F32)],
        compiler_params=_params("parallel", "arbitrary"),
    )(*([proj] * 11), w2_pad, gb, og, tri)


def _out_proj_kernel(h_ref, f_ref, c_ref, g_ref, w_ref, o_ref):
    acc = _dot(f_ref[...], w_ref[0:FOX_W, :])
    acc = acc + _dot(c_ref[...], w_ref[FOX_W:FOX_W + CONV_W, :])
    acc = acc + _dot(g_ref[...], w_ref[FOX_W + CONV_W:, :])
    o_ref[...] = h_ref[...] + acc


def out_proj(h, fox, conv, gla_o, w, tm):
    t, d = h.shape
    row = lambda w_: pl.BlockSpec((tm, w_), lambda i: (i, 0))
    return pl.pallas_call(
        _out_proj_kernel,
        grid=(t // tm,),
        in_specs=[row(d), row(FOX_W), row(CONV_W), row(GLA_VW), pl.BlockSpec(w.shape, lambda i: (0, 0))],
        out_specs=row(d),
        out_shape=jax.ShapeDtypeStruct((t, d), F32),
        compiler_params=_params("parallel"),
    )(h, fox, conv, gla_o, w)


def _mem_kv_kernel(mem_ref, mg_ref, wk_ref, wv_ref, kg_ref, k_ref, v_ref):
    mn = _rms(mem_ref[...], mg_ref[...]).astype(BF16)
    k = _dot(mn, wk_ref[...])
    v_ref[...] = _dot(mn, wv_ref[...]).astype(v_ref.dtype)
    for h in range(XA_HEADS):
        hs = slice(h * XA_HEAD_DIM, (h + 1) * XA_HEAD_DIM)
        k_ref[:, hs] = _rms(k[:, hs], kg_ref[...]).astype(k_ref.dtype)


def mem_kv(mem, mem_g, wk, wv, kg):
    bsz, m, d = mem.shape
    nl, _, xw = wk.shape
    out = jax.ShapeDtypeStruct((nl, bsz, m, xw), BF16)
    o_spec = pl.BlockSpec((None, None, m, xw), lambda l, bi: (l, bi, 0, 0))
    w_spec = pl.BlockSpec((None, d, xw), lambda l, bi: (l, 0, 0))
    return pl.pallas_call(
        _mem_kv_kernel,
        grid=(nl, bsz),
        in_specs=[pl.BlockSpec((None, m, d), lambda l, bi: (bi, 0, 0)),
                  pl.BlockSpec((1, d), lambda l, bi: (0, 0)), w_spec, w_spec,
                  pl.BlockSpec((None, 1, XA_HEAD_DIM), lambda l, bi: (l, 0, 0))],
        out_specs=[o_spec, o_spec],
        out_shape=[out, out],
        compiler_params=_params("parallel", "parallel"),
    )(mem, mem_g, wk, wv, kg)


def _xattn_kernel(h_ref, g_ref, wq_ref, k_ref, v_ref, qg_ref, wo_ref, o_ref):
    h = h_ref[...]
    q = _dot(_rms(h, g_ref[...]).astype(BF16), wq_ref[...])
    scale = XA_HEAD_DIM ** -0.5
    outs = []
    for hd in range(XA_HEADS):
        hs = slice(hd * XA_HEAD_DIM, (hd + 1) * XA_HEAD_DIM)
        qn = (_rms(q[:, hs], qg_ref[...]) * scale).astype(BF16)
        s = _dot_nt(qn, k_ref[:, hs])
        p = jnp.exp(s - jnp.max(s, axis=-1, keepdims=True))
        denom = jnp.sum(p, axis=-1, keepdims=True)
        outs.append((_dot(p.astype(BF16), v_ref[:, hs]) / denom).astype(BF16))
    o_ref[...] = h + _dot(jnp.concatenate(outs, axis=-1), wo_ref[...])


def cross_attention(h, g, wq, k, v, qg, wo, tm):
    bsz, s, d = h.shape
    m, xw = k.shape[-2:]
    row = pl.BlockSpec((None, tm, d), lambda bi, i: (bi, i, 0))
    const = lambda shape: pl.BlockSpec(shape, lambda bi, i: (0, 0))
    kv = pl.BlockSpec((None, m, xw), lambda bi, i: (bi, 0, 0))
    return pl.pallas_call(
        _xattn_kernel,
        grid=(bsz, s // tm),
        in_specs=[row, const((1, d)), const((d, xw)), kv, kv, const((1, XA_HEAD_DIM)), const((xw, d))],
        out_specs=row,
        out_shape=jax.ShapeDtypeStruct((bsz, s, d), F32),
        compiler_params=_params("parallel", "parallel"),
    )(h, g, wq, k, v, qg, wo)


def _ffn_body(xb_ref, wg_ref, wu_ref, wd_ref, acc_ref):
    xb = xb_ref[...]
    a = (_silu(_dot(xb, wg_ref[...])) * _dot(xb, wu_ref[...])).astype(BF16)
    acc_ref[...] += _dot(a, wd_ref[...])


def _dense_ffn_kernel(h_ref, g_ref, wg_ref, wu_ref, wd_ref, o_ref, xb_ref, acc_ref):
    j = pl.program_id(1)

    @pl.when(j == 0)
    def _():
        xb_ref[...] = _rms(h_ref[...], g_ref[...]).astype(BF16)
        acc_ref[...] = jnp.zeros_like(acc_ref)

    _ffn_body(xb_ref, wg_ref, wu_ref, wd_ref, acc_ref)

    @pl.when(j == pl.num_programs(1) - 1)
    def _():
        o_ref[...] = h_ref[...] + acc_ref[...]


def dense_ffn(h, g, wg, wu, wd, tm, tf):
    t, d = h.shape
    f = wg.shape[1]
    row = pl.BlockSpec((tm, d), lambda i, j: (i, 0))
    return pl.pallas_call(
        _dense_ffn_kernel,
        grid=(t // tm, f // tf),
        in_specs=[row, pl.BlockSpec((1, d), lambda i, j: (0, 0)),
                  pl.BlockSpec((d, tf), lambda i, j: (0, j)), pl.BlockSpec((d, tf), lambda i, j: (0, j)),
                  pl.BlockSpec((tf, d), lambda i, j: (j, 0))],
        out_specs=row,
        out_shape=jax.ShapeDtypeStruct((t, d), F32),
        scratch_shapes=[pltpu.VMEM((tm, d), BF16), pltpu.VMEM((tm, d), F32)],
        compiler_params=_params("parallel", "arbitrary"),
    )(h, g, wg, wu, wd)


def _expert_ffn_kernel(te_ref, nu_ref, x_ref, wg_ref, wu_ref, wd_ref, o_ref, xb_ref, acc_ref):
    i, j = pl.program_id(0), pl.program_id(1)
    used = i < nu_ref[0]

    @pl.when(used & (j == 0))
    def _():
        xb_ref[...] = x_ref[...].astype(BF16)
        acc_ref[...] = jnp.zeros_like(acc_ref)

    @pl.when(used)
    def _():
        _ffn_body(xb_ref, wg_ref, wu_ref, wd_ref, acc_ref)

    @pl.when(used & (j == pl.num_programs(1) - 1))
    def _():
        o_ref[...] = acc_ref[...]

    @pl.when(jnp.logical_not(used) & (j == 0))
    def _():
        o_ref[...] = jnp.zeros_like(o_ref)


def expert_ffn(xs, tile_expert, n_used, wg, wu, wd, tm, tf):
    r, d = xs.shape
    f = wg.shape[-1]
    nj = f // tf
    ti = lambda i, nu: jnp.minimum(i, nu[0] - 1)
    tj = lambda i, j, nu: jnp.where(i < nu[0], j, nj - 1)
    row = pl.BlockSpec((tm, d), lambda i, j, te, nu: (ti(i, nu), 0))
    w_in = pl.BlockSpec((None, d, tf), lambda i, j, te, nu: (te[ti(i, nu)], 0, tj(i, j, nu)))
    w_dn = pl.BlockSpec((None, tf, d), lambda i, j, te, nu: (te[ti(i, nu)], tj(i, j, nu), 0))
    return pl.pallas_call(
        _expert_ffn_kernel,
        grid_spec=pltpu.PrefetchScalarGridSpec(
            num_scalar_prefetch=2,
            grid=(r // tm, nj),
            in_specs=[row, w_in, w_in, w_dn],
            out_specs=pl.BlockSpec((tm, d), lambda i, j, te, nu: (i, 0)),
            scratch_shapes=[pltpu.VMEM((tm, d), BF16), pltpu.VMEM((tm, d), F32)],
        ),
        out_shape=jax.ShapeDtypeStruct((r, d), F32),
        compiler_params=_params("arbitrary", "arbitrary"),
    )(tile_expert, n_used, xs, wg, wu, wd)


def _router_kernel(h_ref, g_ref, rw_ref, rb_ref, tri_ref, hn_ref, info_ref, cnt_ref, carry_ref):
    @pl.when(pl.program_id(0) == 0)
    def _():
        carry_ref[...] = jnp.zeros_like(carry_ref)

    hn = _rms(h_ref[...], g_ref[...])
    hn_ref[...] = hn
    tm = hn.shape[0]
    x3 = _split3(hn)
    w3 = _split3(rw_ref[...])
    logits = rb_ref[...]
    for a, bpart in ((2, 0), (1, 1), (0, 2), (1, 0), (0, 1), (0, 0)):
        logits = logits + _dot(x3[a], w3[bpart])
    lane = lax.broadcasted_iota(jnp.int32, (tm, LANES), 1).astype(F32)
    logits = jnp.where(lane < N_EXPERTS, logits, NEG_BIG)
    v_a = jnp.max(logits, axis=-1, keepdims=True)
    e_a = jnp.min(jnp.where(logits == v_a, lane, float(LANES)), axis=-1, keepdims=True)
    rest = jnp.where(lane == e_a, NEG_BIG, logits)
    v_b = jnp.max(rest, axis=-1, keepdims=True)
    e_b = jnp.min(jnp.where(rest == v_b, lane, float(LANES)), axis=-1, keepdims=True)
    t = jnp.exp(v_b - v_a)
    w_a = 1.0 / (1.0 + t)
    w_b = t * w_a
    chosen = (lane == e_a) | (lane == e_b)
    onehot = jnp.where(chosen, 1.0, 0.0)
    before = _dot(tri_ref[...], onehot.astype(BF16)) + carry_ref[...]
    carry_ref[...] += jnp.sum(onehot, axis=0, keepdims=True)
    cnt_ref[...] = carry_ref[...]
    rank_a = jnp.sum(jnp.where(lane == e_a, before, 0.0), axis=-1, keepdims=True)
    rank_b = jnp.sum(jnp.where(lane == e_b, before, 0.0), axis=-1, keepdims=True)
    info_ref[...] = jnp.where(lane == 0, e_a, jnp.where(lane == 1, e_b, jnp.where(
        lane == 2, w_a, jnp.where(lane == 3, w_b, jnp.where(lane == 4, rank_a, jnp.where(lane == 5, rank_b, 0.0))))))


def router(h, g, rw_pad, rb_pad, tm):
    t, d = h.shape
    tri = jnp.tril(jnp.ones((tm, tm), F32), -1).astype(BF16)
    row = lambda w: pl.BlockSpec((tm, w), lambda i: (i, 0))
    const = lambda shape: pl.BlockSpec(shape, lambda i: (0, 0))
    return pl.pallas_call(
        _router_kernel,
        grid=(t // tm,),
        in_specs=[row(d), const((1, d)), const((d, LANES)), const((1, LANES)), const((tm, tm))],
        out_specs=[row(d), row(LANES), const((1, LANES))],
        out_shape=[jax.ShapeDtypeStruct((t, d), F32), jax.ShapeDtypeStruct((t, LANES), F32),
                   jax.ShapeDtypeStruct((1, LANES), F32)],
        scratch_shapes=[pltpu.VMEM((1, LANES), F32)],
        compiler_params=_params("arbitrary"),
    )(h, g, rw_pad, rb_pad, tri)


def _row_copy(src_ref, src_row, dst_ref, dst_row, sem):
    return pltpu.make_async_copy(src_ref.at[pl.ds(src_row, 1), :], dst_ref.at[pl.ds(dst_row, 1), :], sem)


def _dispatch_kernel(da_ref, db_ref, x_ref, init_ref, xs_ref, sem):
    del init_ref
    tm = x_ref.shape[0]
    base = pl.program_id(0) * tm

    def issue(r, _):
        _row_copy(x_ref, r, xs_ref, da_ref[base + r], sem).start()
        _row_copy(x_ref, r, xs_ref, db_ref[base + r], sem).start()
        return 0

    def drain(r, _):
        _row_copy(x_ref, 0, xs_ref, 0, sem).wait()
        _row_copy(x_ref, 0, xs_ref, 0, sem).wait()
        return 0

    lax.fori_loop(0, tm, issue, 0)
    lax.fori_loop(0, tm, drain, 0)


def dispatch(x, dest_a, dest_b, n_rows, tm):
    t, d = x.shape
    return pl.pallas_call(
        _dispatch_kernel,
        grid_spec=pltpu.PrefetchScalarGridSpec(
            num_scalar_prefetch=2,
            grid=(t // tm,),
            in_specs=[pl.BlockSpec((tm, d), lambda i, da, db: (i, 0)), pl.BlockSpec(memory_space=pl.ANY)],
            out_specs=pl.BlockSpec(memory_space=pl.ANY),
            scratch_shapes=[pltpu.SemaphoreType.DMA(())],
        ),
        out_shape=jax.ShapeDtypeStruct((n_rows, d), x.dtype),
        input_output_aliases={3: 0},
        compiler_params=_params("arbitrary"),
    )(dest_a, dest_b, x, jnp.zeros((n_rows, d), x.dtype))


def _combine_kernel(da_ref, db_ref, h_ref, info_ref, ys_ref, o_ref, ya_ref, yb_ref, sem):
    tm = h_ref.shape[0]
    base = pl.program_id(0) * tm

    def issue(r, _):
        _row_copy(ys_ref, da_ref[base + r], ya_ref, r, sem).start()
        _row_copy(ys_ref, db_ref[base + r], yb_ref, r, sem).start()
        return 0

    def drain(r, _):
        _row_copy(ys_ref, 0, ya_ref, 0, sem).wait()
        _row_copy(ys_ref, 0, yb_ref, 0, sem).wait()
        return 0

    lax.fori_loop(0, tm, issue, 0)
    lax.fori_loop(0, tm, drain, 0)
    info = info_ref[...]
    o_ref[...] = h_ref[...] + info[:, 2:3] * ya_ref[...] + info[:, 3:4] * yb_ref[...]


def combine(h, info, ys, dest_a, dest_b, tm):
    t, d = h.shape
    row = lambda w: pl.BlockSpec((tm, w), lambda i, da, db: (i, 0))
    return pl.pallas_call(
        _combine_kernel,
        grid_spec=pltpu.PrefetchScalarGridSpec(
            num_scalar_prefetch=2,
            grid=(t // tm,),
            in_specs=[row(d), row(LANES), pl.BlockSpec(memory_space=pl.ANY)],
            out_specs=row(d),
            scratch_shapes=[pltpu.VMEM((tm, d), F32), pltpu.VMEM((tm, d), F32), pltpu.SemaphoreType.DMA(())],
        ),
        out_shape=jax.ShapeDtypeStruct((t, d), F32),
        compiler_params=_params("arbitrary"),
    )(dest_a, dest_b, h, info, ys)


def moe_ffn(h, g, rw, rb, wg, wu, wd, tm_route, tm_rows, tf, tm_move):
    t, d = h.shape
    rw_pad = jnp.zeros((d, LANES), F32).at[:, :N_EXPERTS].set(rw)
    rb_pad = jnp.zeros((1, LANES), F32).at[0, :N_EXPERTS].set(rb)
    hn, info, counts = router(h, g, rw_pad, rb_pad, tm_route)
    counts = counts[0, :N_EXPERTS].astype(jnp.int32)
    padded = (counts + tm_rows - 1) // tm_rows * tm_rows
    ends = jnp.cumsum(padded)
    starts = ends - padded
    e_a, e_b = info[:, 0].astype(jnp.int32), info[:, 1].astype(jnp.int32)
    dest_a = starts[e_a] + info[:, 4].astype(jnp.int32)
    dest_b = starts[e_b] + info[:, 5].astype(jnp.int32)
    n_rows = 2 * t + N_EXPERTS * tm_rows
    n_tiles = n_rows // tm_rows
    n_used = (ends[-1] // tm_rows).astype(jnp.int32).reshape(1)
    tile_start = jnp.arange(n_tiles, dtype=jnp.int32) * tm_rows
    tile_expert = jnp.minimum(jnp.sum(tile_start[:, None] >= ends[None, :], axis=-1), N_EXPERTS - 1).astype(jnp.int32)
    xs = dispatch(hn, dest_a, dest_b, n_rows, tm_move)
    ys = expert_ffn(xs, tile_expert, n_used, wg, wu, wd, tm_rows, tf)
    return combine(h, info, ys, dest_a, dest_b, tm_move)


def _pack_w_in(w):
    d = w.shape[0]
    pts = [0, 512, 1024, 1536, 1544, 2056, 2568, 3080, 3592, 4616, 4632, 5656]
    fq, fk, fv, ff, ca, cg, gq, gk, gv, glr, gr = [w[:, a:b] for a, b in zip(pts[:-1], pts[1:])]
    pad = jnp.zeros((d, LANES - FOX_HEADS - GLA_GATE_RANK), w.dtype)
    return jnp.concatenate([fq, fk, fv, ca, cg, gq, gk, gv, gr, ff, glr, pad], axis=1).astype(BF16)


def _tile(n, pref):
    return min(n, pref)


def kernel(x, mem, mem_norm, mix_norm, w_in, fox_forget_bias, fox_q_norm, fox_k_norm, conv_w, conv_b,
           conv_ln_g, conv_ln_b, gla_gate_w2, gla_gate_b, gla_out_norm, w_out, xa_norm, xa_wq, xa_wk, xa_wv,
           xa_q_norm, xa_k_norm, xa_wo, ffn_norm, dense_wg, dense_wu, dense_wd, router_w, router_b,
           moe_wg, moe_wu, moe_wd):
    bsz, s, d = x.shape
    t = bsz * s
    depth = mix_norm.shape[0]
    ts = _tile(s, 512)
    row = lambda v: v.reshape(1, -1).astype(F32)

    k_mem, v_mem = mem_kv(mem, row(mem_norm), xa_wk.astype(BF16), xa_wv.astype(BF16),
                          xa_k_norm.reshape(depth, 1, XA_HEAD_DIM))
    h = x.reshape(t, d)
    for l in range(depth):
        proj = norm_matmul(h, row(mix_norm[l]), _pack_w_in(w_in[l]), _tile(t, 1024), PACK_W // 5)
        proj = proj.reshape(bsz, s, PACK_W)
        bf_pad = jnp.zeros((1, LANES), F32).at[0, :FOX_HEADS].set(fox_forget_bias[l])
        qg, kg = row(fox_q_norm[l]), row(fox_k_norm[l])
        q_aug, k_aug, v_aug, c_first, c_last = fox_prep(proj, bf_pad, qg, kg, ts)
        jlo = fox_first_block(c_first, c_last, qg, kg)
        fox = fox_flash(q_aug, k_aug, v_aug, jlo, ts)
        fox = fox.transpose(0, 2, 1, 3).reshape(t, FOX_W)
        conv_w_pad = jnp.zeros((CONV_HALO, CONV_W), F32).at[:CONV_TAPS].set(conv_w[l])
        conv = conformer_conv(proj, conv_w_pad, row(conv_b[l]), row(conv_ln_g[l]), row(conv_ln_b[l]), ts)
        w2_pad = jnp.zeros((LANES, GLA_KW), F32).at[FOX_HEADS:FOX_HEADS + GLA_GATE_RANK].set(gla_gate_w2[l])
        gla_o = gla(proj, w2_pad.astype(BF16), row(gla_gate_b[l]), row(gla_out_norm[l]), ts)
        h = out_proj(h, fox, conv.reshape(t, CONV_W), gla_o.reshape(t, GLA_VW), w_out[l].astype(BF16),
                     _tile(t, 512))
        h = cross_attention(h.reshape(bsz, s, d), row(xa_norm[l]), xa_wq[l].astype(BF16), k_mem[l], v_mem[l],
                            row(xa_q_norm[l]), xa_wo[l].astype(BF16), ts).reshape(t, d)
        j = l // 2
        if l % 2 == 0:
            h = dense_ffn(h, row(ffn_norm[l]), dense_wg[j].astype(BF16), dense_wu[j].astype(BF16),
                          dense_wd[j].astype(BF16), _tile(t, 512), 512)
        else:
            h = moe_ffn(h, row(ffn_norm[l]), router_w[j], router_b[j], moe_wg[j].astype(BF16),
                        moe_wu[j].astype(BF16), moe_wd[j].astype(BF16), _tile(t, 512), _tile(t, 512), 512,
                        _tile(t, 256))
    return h.reshape(bsz, s, d)
```

```python
import functools

import jax
import jax.numpy as jnp
from jax import lax
from jax.experimental import pallas as pl
from jax.experimental.pallas import tpu as pltpu

F32 = jnp.float32
BF16 = jnp.bfloat16
EPS = 1e-6

FOX_HEADS = 8
FOX_HEAD_DIM = 64
CONV_TAPS = 31
CONV_HALO = 32
GLA_HEADS = 4
GLA_DK = 128
GLA_DV = 256
GLA_GATE_RANK = 16
GLA_GATE_TAU = 16.0
GLA_CHUNK = 64
XA_HEADS = 4
XA_HEAD_DIM = 128
N_EXPERTS = 8
LANES = 128
SUBLANES = 8

FOX_W = FOX_HEADS * FOX_HEAD_DIM
CONV_W = 512
GLA_KW = GLA_HEADS * GLA_DK
GLA_VW = GLA_HEADS * GLA_DV
COL_FQ, COL_FK, COL_FV = 0, 512, 1024
COL_CA, COL_CG = 1536, 2048
COL_GQ, COL_GK, COL_GV, COL_GR = 2560, 3072, 3584, 4608
COL_SMALL = 5632
PACK_W = COL_SMALL + LANES

FOX_SKIP_MARGIN = 110.0
NEG_BIG = -1e30

VMEM_LIMIT = 56 * 1024 * 1024


def _params(*sem):
    return pltpu.CompilerParams(dimension_semantics=sem, vmem_limit_bytes=VMEM_LIMIT)


def _dot(a, b):
    return jnp.dot(a, b, preferred_element_type=F32)


def _dot_nt(a, b):
    return lax.dot_general(a, b, (((1,), (1,)), ((), ())), preferred_element_type=F32)


def _split3(x):
    hi = x.astype(BF16)
    r = x - hi.astype(F32)
    mid = r.astype(BF16)
    lo = (r - mid.astype(F32)).astype(BF16)
    return hi, mid, lo


def _log_sigmoid(z):
    return jnp.minimum(z, 0.0) - jnp.log1p(jnp.exp(-jnp.abs(z)))


def _silu(z):
    return z * (1.0 / (1.0 + jnp.exp(-z)))


def _rms(x, g):
    ms = jnp.mean(x * x, axis=-1, keepdims=True)
    return x * lax.rsqrt(ms + EPS) * g


def _norm_mm_kernel(h_ref, g_ref, w_ref, o_ref, hn_ref):
    @pl.when(pl.program_id(1) == 0)
    def _():
        hn_ref[...] = _rms(h_ref[...], g_ref[...]).astype(BF16)

    o_ref[...] = _dot(hn_ref[...], w_ref[...]).astype(o_ref.dtype)


def norm_matmul(h, g, w, tm, tn):
    t, d = h.shape
    n = w.shape[1]
    return pl.pallas_call(
        _norm_mm_kernel,
        grid=(t // tm, n // tn),
        in_specs=[pl.BlockSpec((tm, d), lambda i, j: (i, 0)),
                  pl.BlockSpec((1, d), lambda i, j: (0, 0)),
                  pl.BlockSpec((d, tn), lambda i, j: (0, j))],
        out_specs=pl.BlockSpec((tm, tn), lambda i, j: (i, j)),
        out_shape=jax.ShapeDtypeStruct((t, n), BF16),
        scratch_shapes=[pltpu.VMEM((tm, d), BF16)],
        compiler_params=_params("parallel", "arbitrary"),
    )(h, g, w)


LOG2E = 1.4426950408889634


def _fox_prep_kernel(q_ref, k_ref, v_ref, sm_ref, bf_ref, qg_ref, kg_ref, tri_ref,
                     qa_ref, ka_ref, va_ref, cf_ref, cl_ref, carry_ref):
    ts = q_ref.shape[0]

    @pl.when(pl.program_id(1) == 0)
    def _():
        carry_ref[...] = jnp.zeros_like(carry_ref)

    lf = _log_sigmoid(sm_ref[...].astype(F32) + bf_ref[...])
    hi, mid, lo = _split3(lf)
    tri = tri_ref[...]
    cum = _dot(tri, hi) + _dot(tri, mid) + _dot(tri, lo) + carry_ref[...]
    carry_ref[...] = cum[ts - 1:ts, :]
    cf_ref[...] = cum[0:1, :]
    cl_ref[...] = cum[ts - 1:ts, :]

    lane = lax.broadcasted_iota(jnp.int32, (ts, FOX_HEAD_DIM), 1)
    scale = FOX_HEAD_DIM ** -0.5 * LOG2E
    cum2 = cum * LOG2E
    for h in range(FOX_HEADS):
        sl = slice(h * FOX_HEAD_DIM, (h + 1) * FOX_HEAD_DIM)
        c = cum2[:, h:h + 1]
        c_hi = c.astype(BF16).astype(F32)
        r = c - c_hi
        c_mid = r.astype(BF16).astype(F32)
        c_lo = r - c_mid
        qn = _rms(q_ref[:, sl].astype(F32), qg_ref[...]) * scale
        kn = _rms(k_ref[:, sl].astype(F32), kg_ref[...])
        q_extra = jnp.where(lane == 0, c_hi, jnp.where(lane == 1, c_mid, jnp.where(
            lane == 2, c_lo, jnp.where(lane < 6, 1.0, 0.0))))
        k_extra = jnp.where(lane < 3, 1.0, jnp.where(lane == 3, -c_hi, jnp.where(
            lane == 4, -c_mid, jnp.where(lane == 5, -c_lo, 0.0))))
        v_extra = jnp.where(lane == 0, 1.0, 0.0)
        qa_ref[h] = jnp.concatenate([qn, q_extra], axis=-1).astype(BF16)
        ka_ref[h] = jnp.concatenate([kn, k_extra], axis=-1).astype(BF16)
        va_ref[h] = jnp.concatenate([v_ref[:, sl].astype(F32), v_extra], axis=-1).astype(BF16)


def fox_prep(proj, bf_pad, qg, kg, ts):
    b, s, _ = proj.shape
    nt = s // ts
    tri = jnp.tril(jnp.ones((ts, ts), F32)).astype(BF16)
    aug = jax.ShapeDtypeStruct((b, FOX_HEADS, s, LANES), BF16)
    edge = jax.ShapeDtypeStruct((b, nt, 1, LANES), F32)
    col = lambda c, w: pl.BlockSpec((None, ts, w), lambda bi, i: (bi, i, c // w))
    aug_spec = pl.BlockSpec((None, FOX_HEADS, ts, LANES), lambda bi, i: (bi, 0, i, 0))
    edge_spec = pl.BlockSpec((None, None, 1, LANES), lambda bi, i: (bi, i, 0, 0))
    const = lambda shape: pl.BlockSpec(shape, lambda bi, i: (0,) * len(shape))
    return pl.pallas_call(
        _fox_prep_kernel,
        grid=(b, nt),
        in_specs=[col(COL_FQ, FOX_W), col(COL_FK, FOX_W), col(COL_FV, FOX_W), col(COL_SMALL, LANES),
                  const((1, LANES)), const((1, FOX_HEAD_DIM)), const((1, FOX_HEAD_DIM)), const((ts, ts))],
        out_specs=[aug_spec, aug_spec, aug_spec, edge_spec, edge_spec],
        out_shape=[aug, aug, aug, edge, edge],
        scratch_shapes=[pltpu.VMEM((1, LANES), F32)],
        compiler_params=_params("parallel", "arbitrary"),
    )(proj, proj, proj, proj, bf_pad, qg, kg, tri)


def _fox_flash_kernel(jlo_ref, q_ref, k_ref, v_ref, o_ref):
    tq = q_ref.shape[0] // 2
    i = pl.program_id(2)
    flat = (pl.program_id(0) * pl.num_programs(1) + pl.program_id(1)) * pl.num_programs(2) + i
    q_a, q_b = q_ref[0:tq, :], q_ref[tq:2 * tq, :]
    qpos = lax.broadcasted_iota(jnp.int32, (tq, tq), 0)
    kpos = lax.broadcasted_iota(jnp.int32, (tq, tq), 1)

    def step(q, j, carry, diagonal):
        m, acc = carry
        rows = pl.ds(pl.multiple_of(j * tq, tq), tq)
        s = _dot_nt(q, k_ref[rows, :])
        if diagonal:
            s = jnp.where(kpos <= qpos, s, NEG_BIG)
        m_new = jnp.maximum(m, jnp.max(s, axis=-1, keepdims=True))
        p = jnp.exp2(s - m_new).astype(BF16)
        return m_new, jnp.exp2(m - m_new) * acc + _dot(p, v_ref[rows, :])

    def body(j, carry):
        return step(q_a, j, carry[0], False), step(q_b, j, carry[1], False)

    init = (jnp.full((tq, 1), NEG_BIG, F32), jnp.zeros((tq, LANES), F32))
    c_a, c_b = lax.fori_loop(jlo_ref[flat], 2 * i, body, (init, init))
    c_a, c_b = step(q_a, 2 * i, c_a, True), step(q_b, 2 * i, c_b, False)
    c_b = step(q_b, 2 * i + 1, c_b, True)
    for half, (_, acc) in enumerate((c_a, c_b)):
        out = acc[:, :FOX_HEAD_DIM] / acc[:, FOX_HEAD_DIM:FOX_HEAD_DIM + 1]
        o_ref[half * tq:(half + 1) * tq, :] = out.astype(o_ref.dtype)


def fox_flash(q_aug, k_aug, v_aug, jlo_pair, tq):
    b, nh, s, _ = q_aug.shape
    full = pl.BlockSpec((None, None, s, LANES), lambda bi, h, i, jl: (bi, h, 0, 0))
    return pl.pallas_call(
        _fox_flash_kernel,
        grid_spec=pltpu.PrefetchScalarGridSpec(
            num_scalar_prefetch=1,
            grid=(b, nh, s // (2 * tq)),
            in_specs=[pl.BlockSpec((None, None, 2 * tq, LANES), lambda bi, h, i, jl: (bi, h, i, 0)), full, full],
            out_specs=pl.BlockSpec((None, None, 2 * tq, FOX_HEAD_DIM), lambda bi, h, i, jl: (bi, h, i, 0)),
        ),
        out_shape=jax.ShapeDtypeStruct((b, nh, s, FOX_HEAD_DIM), BF16),
        compiler_params=_params("parallel", "parallel", "arbitrary"),
    )(jlo_pair, q_aug, k_aug, v_aug)


def fox_first_block(c_first, c_last, qg, kg):
    cf = c_first[:, :, 0, :FOX_HEADS].transpose(0, 2, 1)
    cl = c_last[:, :, 0, :FOX_HEADS].transpose(0, 2, 1)
    nq = cf.shape[-1]
    qk_bound = 2.0 * FOX_HEAD_DIM ** 0.5 * jnp.max(jnp.abs(qg)) * jnp.max(jnp.abs(kg)) * (1.0 + 1e-3)
    gap = cf[..., :, None] - cl[..., None, :]
    j_idx = jnp.arange(nq)
    skippable = (qk_bound + gap < -FOX_SKIP_MARGIN) & (j_idx[None, :] < j_idx[:, None])
    first = jnp.sum(skippable, axis=-1).astype(jnp.int32)
    return jnp.minimum(first[..., 0::2], first[..., 1::2]).reshape(-1)


def _conv_kernel(a_ref, g_ref, ap_ref, gp_ref, w_ref, b_ref, lg_ref, lb_ref, o_ref, u_ref):
    ts = a_ref.shape[0]

    def glu(a, g):
        g = g.astype(F32)
        return a.astype(F32) * (1.0 / (1.0 + jnp.exp(-g)))

    prev = glu(ap_ref[...], gp_ref[...])
    u_ref[0, 0:CONV_HALO, :] = jnp.where(pl.program_id(1) > 0, prev, 0.0)
    u_ref[0, CONV_HALO:CONV_HALO + ts, :] = glu(a_ref[...], g_ref[...])
    u_ref[0, CONV_HALO + ts:, :] = jnp.zeros((SUBLANES, CONV_W), F32)
    base = u_ref[0]
    for r in range(1, SUBLANES):
        u_ref[r] = pltpu.roll(base, base.shape[0] - r, axis=0)
    first = CONV_HALO - (CONV_TAPS - 1)
    y = jnp.zeros((ts, CONV_W), F32) + b_ref[...]
    for j in range(CONV_TAPS):
        r, start = (first + j) % SUBLANES, (first + j) // SUBLANES * SUBLANES
        y = y + w_ref[j:j + 1, :] * u_ref[r, start:start + ts, :]
    mu = jnp.mean(y, axis=-1, keepdims=True)
    yc = y - mu
    var = jnp.mean(yc * yc, axis=-1, keepdims=True)
    y = yc * lax.rsqrt(var + EPS) * lg_ref[...] + lb_ref[...]
    o_ref[...] = _silu(y).astype(o_ref.dtype)


def conformer_conv(proj, w_pad, b, ln_g, ln_b, ts):
    bsz, s, _ = proj.shape
    ratio = ts // CONV_HALO
    main = lambda c: pl.BlockSpec((None, ts, CONV_W), lambda bi, i: (bi, i, c // CONV_W))
    halo = lambda c: pl.BlockSpec((None, CONV_HALO, CONV_W),
                                  lambda bi, i: (bi, jnp.maximum(i * ratio - 1, 0), c // CONV_W))
    const = lambda shape: pl.BlockSpec(shape, lambda bi, i: (0, 0))
    return pl.pallas_call(
        _conv_kernel,
        grid=(bsz, s // ts),
        in_specs=[main(COL_CA), main(COL_CG), halo(COL_CA), halo(COL_CG),
                  const((CONV_HALO, CONV_W)), const((1, CONV_W)), const((1, CONV_W)), const((1, CONV_W))],
        out_specs=pl.BlockSpec((None, ts, CONV_W), lambda bi, i: (bi, i, 0)),
        out_shape=jax.ShapeDtypeStruct((bsz, s, CONV_W), BF16),
        scratch_shapes=[pltpu.VMEM((SUBLANES, CONV_HALO + ts + SUBLANES, CONV_W), F32)],
        compiler_params=_params("parallel", "parallel"),
    )(proj, proj, proj, proj, w_pad, b, ln_g, ln_b)


def _gla_kernel(q_ref, k_ref, v0_ref, v1_ref, v2_ref, v3_ref, r0_ref, r1_ref, r2_ref, r3_ref,
                sm_ref, w2_ref, gb_ref, og_ref, tri_ref, o_ref, state_ref, bcum_ref):
    ts = q_ref.shape[0]
    c_len = GLA_CHUNK
    v_refs = (v0_ref, v1_ref, v2_ref, v3_ref)
    r_refs = (r0_ref, r1_ref, r2_ref, r3_ref)

    @pl.when(pl.program_id(1) == 0)
    def _():
        state_ref[...] = jnp.zeros_like(state_ref)

    log_a = _log_sigmoid(_dot(sm_ref[...], w2_ref[...]) + gb_ref[...]) * (1.0 / GLA_GATE_TAU)
    hi, mid, lo = _split3(log_a)
    tri = tri_ref[...]
    bcum_ref[...] = _dot(tri, hi) + _dot(tri, mid) + _dot(tri, lo)

    rr = lax.broadcasted_iota(jnp.int32, (c_len, c_len), 0)
    cc = lax.broadcasted_iota(jnp.int32, (c_len, c_len), 1)
    causal = cc <= rr
    scale = GLA_DK ** -0.5

    def chunk(ci, _):
        rows = pl.ds(pl.multiple_of(ci * c_len, c_len), c_len)
        for h in range(GLA_HEADS):
            hs = slice(h * GLA_DK, (h + 1) * GLA_DK)
            bc = bcum_ref[rows, hs]
            b_mid = bc[c_len // 2:c_len // 2 + 1, :]
            b_last = bc[c_len - 1:c_len, :]
            q = q_ref[rows, hs].astype(F32) * scale
            k = k_ref[rows, hs].astype(F32)
            v = v_refs[h][rows, :]
            q_rel = (q * jnp.exp(bc - b_mid)).astype(BF16)
            k_rel = (k * jnp.exp(b_mid - bc)).astype(BF16)
            scores = jnp.where(causal, _dot_nt(q_rel, k_rel), 0.0)
            state = state_ref[h]
            o = _dot(scores.astype(BF16), v) + _dot((q * jnp.exp(bc)).astype(BF16), state.astype(BF16))
            k_out_t = jnp.transpose(k * jnp.exp(b_last - bc))
            decay = jnp.transpose(jnp.broadcast_to(jnp.exp(b_last), (8, GLA_DK)))[:, 0:1]
            state_ref[h] = decay * state + _dot(k_out_t.astype(BF16), v)
            on = _rms(o, og_ref[...])
            o_ref[rows, h * GLA_DV:(h + 1) * GLA_DV] = (on * _silu(r_refs[h][rows, :].astype(F32))).astype(o_ref.dtype)
        return 0

    lax.fori_loop(0, ts // c_len, chunk, 0)


def gla(proj, w2_pad, gb, og, ts):
    bsz, s, _ = proj.shape
    blk = jnp.arange(ts) // GLA_CHUNK
    tri = (jnp.tril(jnp.ones((ts, ts), F32)) * (blk[:, None] == blk[None, :])).astype(BF16)
    col = lambda c, w: pl.BlockSpec((None, ts, w), lambda bi, i: (bi, i, c // w))
    const = lambda shape: pl.BlockSpec(shape, lambda bi, i: (0, 0))
    v_specs = [col(COL_GV + h * GLA_DV, GLA_DV) for h in range(GLA_HEADS)]
    r_specs = [col(COL_GR + h * GLA_DV, GLA_DV) for h in range(GLA_HEADS)]
    return pl.pallas_call(
        _gla_kernel,
        grid=(bsz, s // ts),
        in_specs=[col(COL_GQ, GLA_KW), col(COL_GK, GLA_KW)] + v_specs + r_specs + [
            col(COL_SMALL, LANES), const((LANES, GLA_KW)), const((1, GLA_KW)), const((1, GLA_DV)), const((ts, ts))],
        out_specs=pl.BlockSpec((None, ts, GLA_VW), lambda bi, i: (bi, i, 0)),
        out_shape=jax.ShapeDtypeStruct((bsz, s, GLA_VW), BF16),
        scratch_shapes=[pltpu.VMEM((GLA_HEADS, GLA_DK, GLA_DV), F32), pltpu.VMEM((ts, GLA_KW), F32)],
        compiler_params=_params("parallel", "arbitrary"),
    )(*([proj] * 11), w2_pad, gb, og, tri)


def _out_proj_kernel(h_ref, f_ref, c_ref, g_ref, w_ref, o_ref):
    acc = _dot(f_ref[...], w_ref[0:FOX_W, :])
    acc = acc + _dot(c_ref[...], w_ref[FOX_W:FOX_W + CONV_W, :])
    acc = acc + _dot(g_ref[...], w_ref[FOX_W + CONV_W:, :])
    o_ref[...] = h_ref[...] + acc


def out_proj(h, fox, conv, gla_o, w, tm):
    t, d = h.shape
    row = lambda w_: pl.BlockSpec((tm, w_), lambda i: (i, 0))
    return pl.pallas_call(
        _out_proj_kernel,
        grid=(t // tm,),
        in_specs=[row(d), row(FOX_W), row(CONV_W), row(GLA_VW), pl.BlockSpec(w.shape, lambda i: (0, 0))],
        out_specs=row(d),
        out_shape=jax.ShapeDtypeStruct((t, d), F32),
        compiler_params=_params("parallel"),
    )(h, fox, conv, gla_o, w)


def _mem_kv_kernel(mem_ref, mg_ref, wk_ref, wv_ref, kg_ref, k_ref, v_ref):
    mn = _rms(mem_ref[...], mg_ref[...]).astype(BF16)
    k = _dot(mn, wk_ref[...])
    v_ref[...] = _dot(mn, wv_ref[...]).astype(v_ref.dtype)
    for h in range(XA_HEADS):
        hs = slice(h * XA_HEAD_DIM, (h + 1) * XA_HEAD_DIM)
        k_ref[:, hs] = _rms(k[:, hs], kg_ref[...]).astype(k_ref.dtype)


def mem_kv(mem, mem_g, wk, wv, kg):
    bsz, m, d = mem.shape
    nl, _, xw = wk.shape
    out = jax.ShapeDtypeStruct((nl, bsz, m, xw), BF16)
    o_spec = pl.BlockSpec((None, None, m, xw), lambda l, bi: (l, bi, 0, 0))
    w_spec = pl.BlockSpec((None, d, xw), lambda l, bi: (l, 0, 0))
    return pl.pallas_call(
        _mem_kv_kernel,
        grid=(nl, bsz),
        in_specs=[pl.BlockSpec((None, m, d), lambda l, bi: (bi, 0, 0)),
                  pl.BlockSpec((1, d), lambda l, bi: (0, 0)), w_spec, w_spec,
                  pl.BlockSpec((None, 1, XA_HEAD_DIM), lambda l, bi: (l, 0, 0))],
        out_specs=[o_spec, o_spec],
        out_shape=[out, out],
        compiler_params=_params("parallel", "parallel"),
    )(mem, mem_g, wk, wv, kg)


def _xattn_kernel(h_ref, g_ref, wq_ref, k_ref, v_ref, qg_ref, wo_ref, o_ref):
    h = h_ref[...]
    q = _dot(_rms(h, g_ref[...]).astype(BF16), wq_ref[...])
    scale = XA_HEAD_DIM ** -0.5
    outs = []
    for hd in range(XA_HEADS):
        hs = slice(hd * XA_HEAD_DIM, (hd + 1) * XA_HEAD_DIM)
        qn = (_rms(q[:, hs], qg_ref[...]) * scale).astype(BF16)
        s = _dot_nt(qn, k_ref[:, hs])
        p = jnp.exp(s - jnp.max(s, axis=-1, keepdims=True))
        denom = jnp.sum(p, axis=-1, keepdims=True)
        outs.append((_dot(p.astype(BF16), v_ref[:, hs]) / denom).astype(BF16))
    o_ref[...] = h + _dot(jnp.concatenate(outs, axis=-1), wo_ref[...])


def cross_attention(h, g, wq, k, v, qg, wo, tm):
    bsz, s, d = h.shape
    m, xw = k.shape[-2:]
    row = pl.BlockSpec((None, tm, d), lambda bi, i: (bi, i, 0))
    const = lambda shape: pl.BlockSpec(shape, lambda bi, i: (0, 0))
    kv = pl.BlockSpec((None, m, xw), lambda bi, i: (bi, 0, 0))
    return pl.pallas_call(
        _xattn_kernel,
        grid=(bsz, s // tm),
        in_specs=[row, const((1, d)), const((d, xw)), kv, kv, const((1, XA_HEAD_DIM)), const((xw, d))],
        out_specs=row,
        out_shape=jax.ShapeDtypeStruct((bsz, s, d), F32),
        compiler_params=_params("parallel", "parallel"),
    )(h, g, wq, k, v, qg, wo)


FFN_OUT_CHUNK = 512


def _ffn_body(xb_ref, wg_ref, wu_ref, wd_ref, acc_ref):
    xb = xb_ref[...]
    a = (_silu(_dot(xb, wg_ref[...])) * _dot(xb, wu_ref[...])).astype(BF16)
    for c in range(0, acc_ref.shape[-1], FFN_OUT_CHUNK):
        acc_ref[:, c:c + FFN_OUT_CHUNK] += _dot(a, wd_ref[:, c:c + FFN_OUT_CHUNK])


def _dense_ffn_kernel(h_ref, g_ref, wg_ref, wu_ref, wd_ref, o_ref, xb_ref):
    @pl.when(pl.program_id(1) == 0)
    def _():
        xb_ref[...] = _rms(h_ref[...], g_ref[...]).astype(BF16)
        o_ref[...] = h_ref[...]

    _ffn_body(xb_ref, wg_ref, wu_ref, wd_ref, o_ref)


def dense_ffn(h, g, wg, wu, wd, tm, tf):
    t, d = h.shape
    f = wg.shape[1]
    row = pl.BlockSpec((tm, d), lambda i, j: (i, 0))
    return pl.pallas_call(
        _dense_ffn_kernel,
        grid=(t // tm, f // tf),
        in_specs=[row, pl.BlockSpec((1, d), lambda i, j: (0, 0)),
                  pl.BlockSpec((d, tf), lambda i, j: (0, j)), pl.BlockSpec((d, tf), lambda i, j: (0, j)),
                  pl.BlockSpec((tf, d), lambda i, j: (j, 0))],
        out_specs=pl.BlockSpec((tm, d), lambda i, j: (i, 0), pipeline_mode=pl.Buffered(1)),
        out_shape=jax.ShapeDtypeStruct((t, d), F32),
        scratch_shapes=[pltpu.VMEM((tm, d), BF16)],
        compiler_params=_params("parallel", "arbitrary"),
    )(h, g, wg, wu, wd)


def _expert_ffn_kernel(te_ref, nu_ref, x_ref, wg_ref, wu_ref, wd_ref, o_ref, xb_ref):
    del te_ref

    @pl.when(pl.program_id(1) == 0)
    def _():
        xb_ref[...] = x_ref[...].astype(BF16)
        o_ref[...] = jnp.zeros_like(o_ref)

    @pl.when(pl.program_id(0) < nu_ref[0])
    def _():
        _ffn_body(xb_ref, wg_ref, wu_ref, wd_ref, o_ref)


def expert_ffn(xs, tile_expert, n_used, wg, wu, wd, tm, tf):
    r, d = xs.shape
    f = wg.shape[-1]
    nj = f // tf
    ti = lambda i, nu: jnp.minimum(i, nu[0] - 1)
    tj = lambda i, j, nu: jnp.where(i < nu[0], j, nj - 1)
    row = pl.BlockSpec((tm, d), lambda i, j, te, nu: (ti(i, nu), 0))
    w_in = pl.BlockSpec((None, d, tf), lambda i, j, te, nu: (te[ti(i, nu)], 0, tj(i, j, nu)))
    w_dn = pl.BlockSpec((None, tf, d), lambda i, j, te, nu: (te[ti(i, nu)], tj(i, j, nu), 0))
    return pl.pallas_call(
        _expert_ffn_kernel,
        grid_spec=pltpu.PrefetchScalarGridSpec(
            num_scalar_prefetch=2,
            grid=(r // tm, nj),
            in_specs=[row, w_in, w_in, w_dn],
            out_specs=pl.BlockSpec((tm, d), lambda i, j, te, nu: (i, 0), pipeline_mode=pl.Buffered(1)),
            scratch_shapes=[pltpu.VMEM((tm, d), BF16)],
        ),
        out_shape=jax.ShapeDtypeStruct((r, d), F32),
        compiler_params=_params("arbitrary", "arbitrary"),
    )(tile_expert, n_used, xs, wg, wu, wd)


def _router_kernel(h_ref, g_ref, rw_ref, rb_ref, tri_ref, hn_ref, info_ref, cnt_ref, carry_ref):
    @pl.when(pl.program_id(0) == 0)
    def _():
        carry_ref[...] = jnp.zeros_like(carry_ref)

    hn = _rms(h_ref[...], g_ref[...])
    hn_ref[...] = hn
    tm = hn.shape[0]
    x3 = _split3(hn)
    w3 = _split3(rw_ref[...])
    logits = rb_ref[...]
    for a, bpart in ((2, 0), (1, 1), (0, 2), (1, 0), (0, 1), (0, 0)):
        logits = logits + _dot(x3[a], w3[bpart])
    lane = lax.broadcasted_iota(jnp.int32, (tm, LANES), 1).astype(F32)
    logits = jnp.where(lane < N_EXPERTS, logits, NEG_BIG)
    v_a = jnp.max(logits, axis=-1, keepdims=True)
    e_a = jnp.min(jnp.where(logits == v_a, lane, float(LANES)), axis=-1, keepdims=True)
    rest = jnp.where(lane == e_a, NEG_BIG, logits)
    v_b = jnp.max(rest, axis=-1, keepdims=True)
    e_b = jnp.min(jnp.where(rest == v_b, lane, float(LANES)), axis=-1, keepdims=True)
    t = jnp.exp(v_b - v_a)
    w_a = 1.0 / (1.0 + t)
    w_b = t * w_a
    chosen = (lane == e_a) | (lane == e_b)
    onehot = jnp.where(chosen, 1.0, 0.0)
    before = _dot(tri_ref[...], onehot.astype(BF16)) + carry_ref[...]
    carry_ref[...] += jnp.sum(onehot, axis=0, keepdims=True)
    cnt_ref[...] = carry_ref[...]
    rank_a = jnp.sum(jnp.where(lane == e_a, before, 0.0), axis=-1, keepdims=True)
    rank_b = jnp.sum(jnp.where(lane == e_b, before, 0.0), axis=-1, keepdims=True)
    info_ref[...] = jnp.where(lane == 0, e_a, jnp.where(lane == 1, e_b, jnp.where(
        lane == 2, w_a, jnp.where(lane == 3, w_b, jnp.where(lane == 4, rank_a, jnp.where(lane == 5, rank_b, 0.0))))))


def router(h, g, rw_pad, rb_pad, tm):
    t, d = h.shape
    tri = jnp.tril(jnp.ones((tm, tm), F32), -1).astype(BF16)
    row = lambda w: pl.BlockSpec((tm, w), lambda i: (i, 0))
    const = lambda shape: pl.BlockSpec(shape, lambda i: (0, 0))
    return pl.pallas_call(
        _router_kernel,
        grid=(t // tm,),
        in_specs=[row(d), const((1, d)), const((d, LANES)), const((1, LANES)), const((tm, tm))],
        out_specs=[row(d), row(LANES), const((1, LANES))],
        out_shape=[jax.ShapeDtypeStruct((t, d), F32), jax.ShapeDtypeStruct((t, LANES), F32),
                   jax.ShapeDtypeStruct((1, LANES), F32)],
        scratch_shapes=[pltpu.VMEM((1, LANES), F32)],
        compiler_params=_params("arbitrary"),
    )(h, g, rw_pad, rb_pad, tri)


def _row_copy(src_ref, src_row, dst_ref, dst_row, sem):
    return pltpu.make_async_copy(src_ref.at[pl.ds(src_row, 1), :], dst_ref.at[pl.ds(dst_row, 1), :], sem)


def _dispatch_kernel(da_ref, db_ref, x_ref, init_ref, xs_ref, sem):
    del init_ref
    tm = x_ref.shape[0]
    base = pl.program_id(0) * tm

    def issue(r, _):
        _row_copy(x_ref, r, xs_ref, da_ref[base + r], sem).start()
        _row_copy(x_ref, r, xs_ref, db_ref[base + r], sem).start()
        return 0

    def drain(r, _):
        _row_copy(x_ref, 0, xs_ref, 0, sem).wait()
        _row_copy(x_ref, 0, xs_ref, 0, sem).wait()
        return 0

    lax.fori_loop(0, tm, issue, 0, unroll=8)
    lax.fori_loop(0, tm, drain, 0, unroll=8)


def dispatch(x, dest_a, dest_b, n_rows, tm):
    t, d = x.shape
    return pl.pallas_call(
        _dispatch_kernel,
        grid_spec=pltpu.PrefetchScalarGridSpec(
            num_scalar_prefetch=2,
            grid=(t // tm,),
            in_specs=[pl.BlockSpec((tm, d), lambda i, da, db: (i, 0)), pl.BlockSpec(memory_space=pl.ANY)],
            out_specs=pl.BlockSpec(memory_space=pl.ANY),
            scratch_shapes=[pltpu.SemaphoreType.DMA(())],
        ),
        out_shape=jax.ShapeDtypeStruct((n_rows, d), x.dtype),
        input_output_aliases={3: 0},
        compiler_params=_params("arbitrary"),
    )(dest_a, dest_b, x, jnp.zeros((n_rows, d), x.dtype))


def _combine_kernel(da_ref, db_ref, h_ref, info_ref, ys_ref, o_ref, ya_ref, yb_ref, sem):
    tm = h_ref.shape[0]
    base = pl.program_id(0) * tm

    def issue(r, _):
        _row_copy(ys_ref, da_ref[base + r], ya_ref, r, sem).start()
        _row_copy(ys_ref, db_ref[base + r], yb_ref, r, sem).start()
        return 0

    def drain(r, _):
        _row_copy(ys_ref, 0, ya_ref, 0, sem).wait()
        _row_copy(ys_ref, 0, yb_ref, 0, sem).wait()
        return 0

    lax.fori_loop(0, tm, issue, 0, unroll=8)
    lax.fori_loop(0, tm, drain, 0, unroll=8)
    info = info_ref[...]
    o_ref[...] = h_ref[...] + info[:, 2:3] * ya_ref[...] + info[:, 3:4] * yb_ref[...]


def combine(h, info, ys, dest_a, dest_b, tm):
    t, d = h.shape
    row = lambda w: pl.BlockSpec((tm, w), lambda i, da, db: (i, 0))
    return pl.pallas_call(
        _combine_kernel,
        grid_spec=pltpu.PrefetchScalarGridSpec(
            num_scalar_prefetch=2,
            grid=(t // tm,),
            in_specs=[row(d), row(LANES), pl.BlockSpec(memory_space=pl.ANY)],
            out_specs=row(d),
            scratch_shapes=[pltpu.VMEM((tm, d), F32), pltpu.VMEM((tm, d), F32), pltpu.SemaphoreType.DMA(())],
        ),
        out_shape=jax.ShapeDtypeStruct((t, d), F32),
        compiler_params=_params("arbitrary"),
    )(dest_a, dest_b, h, info, ys)


def moe_ffn(h, g, rw, rb, wg, wu, wd, tm_route, tm_rows, tf, tm_move):
    t, d = h.shape
    rw_pad = jnp.zeros((d, LANES), F32).at[:, :N_EXPERTS].set(rw)
    rb_pad = jnp.zeros((1, LANES), F32).at[0, :N_EXPERTS].set(rb)
    hn, info, counts = router(h, g, rw_pad, rb_pad, tm_route)
    counts = counts[0, :N_EXPERTS].astype(jnp.int32)
    padded = (counts + tm_rows - 1) // tm_rows * tm_rows
    ends = jnp.cumsum(padded)
    starts = ends - padded
    e_a, e_b = info[:, 0].astype(jnp.int32), info[:, 1].astype(jnp.int32)
    dest_a = starts[e_a] + info[:, 4].astype(jnp.int32)
    dest_b = starts[e_b] + info[:, 5].astype(jnp.int32)
    n_rows = 2 * t + N_EXPERTS * tm_rows
    n_tiles = n_rows // tm_rows
    n_used = (ends[-1] // tm_rows).astype(jnp.int32).reshape(1)
    tile_start = jnp.arange(n_tiles, dtype=jnp.int32) * tm_rows
    tile_expert = jnp.minimum(jnp.sum(tile_start[:, None] >= ends[None, :], axis=-1), N_EXPERTS - 1).astype(jnp.int32)
    xs = dispatch(hn, dest_a, dest_b, n_rows, tm_move)
    ys = expert_ffn(xs, tile_expert, n_used, wg, wu, wd, tm_rows, tf)
    return combine(h, info, ys, dest_a, dest_b, tm_move)


def _pack_w_in(w):
    d = w.shape[0]
    pts = [0, 512, 1024, 1536, 1544, 2056, 2568, 3080, 3592, 4616, 4632, 5656]
    fq, fk, fv, ff, ca, cg, gq, gk, gv, glr, gr = [w[:, a:b] for a, b in zip(pts[:-1], pts[1:])]
    pad = jnp.zeros((d, LANES - FOX_HEADS - GLA_GATE_RANK), w.dtype)
    return jnp.concatenate([fq, fk, fv, ca, cg, gq, gk, gv, gr, ff, glr, pad], axis=1).astype(BF16)


def _tile(n, pref):
    return min(n, pref)


def kernel(x, mem, mem_norm, mix_norm, w_in, fox_forget_bias, fox_q_norm, fox_k_norm, conv_w, conv_b,
           conv_ln_g, conv_ln_b, gla_gate_w2, gla_gate_b, gla_out_norm, w_out, xa_norm, xa_wq, xa_wk, xa_wv,
           xa_q_norm, xa_k_norm, xa_wo, ffn_norm, dense_wg, dense_wu, dense_wd, router_w, router_b,
           moe_wg, moe_wu, moe_wd):
    bsz, s, d = x.shape
    t = bsz * s
    depth = mix_norm.shape[0]
    ts = _tile(s, 512)
    row = lambda v: v.reshape(1, -1).astype(F32)

    k_mem, v_mem = mem_kv(mem, row(mem_norm), xa_wk.astype(BF16), xa_wv.astype(BF16),
                          xa_k_norm.reshape(depth, 1, XA_HEAD_DIM))
    h = x.reshape(t, d)
    for l in range(depth):
        proj = norm_matmul(h, row(mix_norm[l]), _pack_w_in(w_in[l]), _tile(t, 1024), PACK_W // 3)
        proj = proj.reshape(bsz, s, PACK_W)
        bf_pad = jnp.zeros((1, LANES), F32).at[0, :FOX_HEADS].set(fox_forget_bias[l])
        qg, kg = row(fox_q_norm[l]), row(fox_k_norm[l])
        q_aug, k_aug, v_aug, c_first, c_last = fox_prep(proj, bf_pad, qg, kg, ts)
        jlo = fox_first_block(c_first, c_last, qg, kg)
        fox = fox_flash(q_aug, k_aug, v_aug, jlo, ts)
        fox = fox.transpose(0, 2, 1, 3).reshape(t, FOX_W)
        conv_w_pad = jnp.zeros((CONV_HALO, CONV_W), F32).at[:CONV_TAPS].set(conv_w[l])
        conv = conformer_conv(proj, conv_w_pad, row(conv_b[l]), row(conv_ln_g[l]), row(conv_ln_b[l]), ts)
        w2_pad = jnp.zeros((LANES, GLA_KW), F32).at[FOX_HEADS:FOX_HEADS + GLA_GATE_RANK].set(gla_gate_w2[l])
        gla_o = gla(proj, w2_pad.astype(BF16), row(gla_gate_b[l]), row(gla_out_norm[l]), ts)
        h = out_proj(h, fox, conv.reshape(t, CONV_W), gla_o.reshape(t, GLA_VW), w_out[l].astype(BF16),
                     _tile(t, 512))
        h = cross_attention(h.reshape(bsz, s, d), row(xa_norm[l]), xa_wq[l].astype(BF16), k_mem[l], v_mem[l],
                            row(xa_q_norm[l]), xa_wo[l].astype(BF16), ts).reshape(t, d)
        j = l // 2
        if l % 2 == 0:
            h = dense_ffn(h, row(ffn_norm[l]), dense_wg[j].astype(BF16), dense_wu[j].astype(BF16),
                          dense_wd[j].astype(BF16), _tile(t, 1024), 512)
        else:
            h = moe_ffn(h, row(ffn_norm[l]), router_w[j], router_b[j], moe_wg[j].astype(BF16),
                        moe_wu[j].astype(BF16), moe_wd[j].astype(BF16), _tile(t, 512), _tile(t, 1024), 512,
                        _tile(t, 256))
    return h.reshape(bsz, s, d)
```

```python
import functools

import jax
import jax.numpy as jnp
from jax import lax
from jax.experimental import pallas as pl
from jax.experimental.pallas import tpu as pltpu

F32 = jnp.float32
BF16 = jnp.bfloat16
EPS = 1e-6

FOX_HEADS = 8
FOX_HEAD_DIM = 64
CONV_TAPS = 31
CONV_HALO = 32
GLA_HEADS = 4
GLA_DK = 128
GLA_DV = 256
GLA_GATE_RANK = 16
GLA_GATE_TAU = 16.0
GLA_CHUNK = 64
XA_HEADS = 4
XA_HEAD_DIM = 128
N_EXPERTS = 8
LANES = 128
SUBLANES = 8

FOX_W = FOX_HEADS * FOX_HEAD_DIM
CONV_W = 512
GLA_KW = GLA_HEADS * GLA_DK
GLA_VW = GLA_HEADS * GLA_DV
COL_FQ, COL_FK, COL_FV = 0, 512, 1024
COL_CA, COL_CG = 1536, 2048
COL_GQ, COL_GK, COL_GV, COL_GR = 2560, 3072, 3584, 4608
COL_SMALL = 5632
PACK_W = COL_SMALL + LANES

FOX_SKIP_MARGIN = 110.0
NEG_BIG = -1e30
FOX_CHAINS = 4
FOX_SHIFT_LIMIT = 100.0

VMEM_LIMIT = 56 * 1024 * 1024


def _params(*sem):
    return pltpu.CompilerParams(dimension_semantics=sem, vmem_limit_bytes=VMEM_LIMIT)


def _dot(a, b):
    return jnp.dot(a, b, preferred_element_type=F32)


def _dot_nt(a, b):
    return lax.dot_general(a, b, (((1,), (1,)), ((), ())), preferred_element_type=F32)


def _split3(x):
    hi = x.astype(BF16)
    r = x - hi.astype(F32)
    mid = r.astype(BF16)
    lo = (r - mid.astype(F32)).astype(BF16)
    return hi, mid, lo


def _log_sigmoid(z):
    return jnp.minimum(z, 0.0) - jnp.log1p(jnp.exp(-jnp.abs(z)))


def _silu(z):
    return z * (1.0 / (1.0 + jnp.exp(-z)))


def _rms(x, g):
    ms = jnp.mean(x * x, axis=-1, keepdims=True)
    return x * lax.rsqrt(ms + EPS) * g


def _norm_mm_kernel(h_ref, g_ref, w_ref, o_ref, hn_ref):
    @pl.when(pl.program_id(1) == 0)
    def _():
        hn_ref[...] = _rms(h_ref[...], g_ref[...]).astype(BF16)

    o_ref[...] = _dot(hn_ref[...], w_ref[...]).astype(o_ref.dtype)


def norm_matmul(h, g, w, tm, tn):
    t, d = h.shape
    n = w.shape[1]
    return pl.pallas_call(
        _norm_mm_kernel,
        grid=(t // tm, n // tn),
        in_specs=[pl.BlockSpec((tm, d), lambda i, j: (i, 0)),
                  pl.BlockSpec((1, d), lambda i, j: (0, 0)),
                  pl.BlockSpec((d, tn), lambda i, j: (0, j))],
        out_specs=pl.BlockSpec((tm, tn), lambda i, j: (i, j)),
        out_shape=jax.ShapeDtypeStruct((t, n), BF16),
        scratch_shapes=[pltpu.VMEM((tm, d), BF16)],
        compiler_params=_params("parallel", "arbitrary"),
    )(h, g, w)


LOG2E = 1.4426950408889634


def _fox_prep_kernel(q_ref, k_ref, v_ref, sm_ref, bf_ref, qg_ref, kg_ref, tri_ref,
                     qa_ref, ka_ref, va_ref, cf_ref, cl_ref, carry_ref, *, shifted):
    ts = q_ref.shape[0]

    @pl.when(pl.program_id(1) == 0)
    def _():
        carry_ref[...] = jnp.zeros_like(carry_ref)

    lf = _log_sigmoid(sm_ref[...].astype(F32) + bf_ref[...])
    hi, mid, lo = _split3(lf)
    tri = tri_ref[...]
    cum = _dot(tri, hi) + _dot(tri, mid) + _dot(tri, lo) + carry_ref[...]
    carry_ref[...] = cum[ts - 1:ts, :]
    cf_ref[...] = cum[0:1, :]
    cl_ref[...] = cum[ts - 1:ts, :]

    lane = lax.broadcasted_iota(jnp.int32, (ts, FOX_HEAD_DIM), 1)
    scale = FOX_HEAD_DIM ** -0.5 * LOG2E
    cum2 = cum * LOG2E
    def split_f32(x):
        hi = x.astype(BF16).astype(F32)
        mid = (x - hi).astype(BF16).astype(F32)
        return hi, mid, x - hi - mid

    for h in range(FOX_HEADS):
        sl = slice(h * FOX_HEAD_DIM, (h + 1) * FOX_HEAD_DIM)
        c = cum2[:, h:h + 1]
        qn = _rms(q_ref[:, sl].astype(F32), qg_ref[...]) * scale
        kn = _rms(k_ref[:, sl].astype(F32), kg_ref[...])
        cq = c - jnp.sum(qn * kn, axis=-1, keepdims=True) if shifted else c
        q_hi, q_mid, q_lo = split_f32(cq)
        c_hi, c_mid, c_lo = split_f32(c)
        q_extra = jnp.where(lane == 0, q_hi, jnp.where(lane == 1, q_mid, jnp.where(
            lane == 2, q_lo, jnp.where(lane < 6, 1.0, 0.0))))
        k_extra = jnp.where(lane < 3, 1.0, jnp.where(lane == 3, -c_hi, jnp.where(
            lane == 4, -c_mid, jnp.where(lane == 5, -c_lo, 0.0))))
        v_extra = jnp.where(lane == 0, 1.0, 0.0)
        qa_ref[h] = jnp.concatenate([qn, q_extra], axis=-1).astype(BF16)
        ka_ref[h] = jnp.concatenate([kn, k_extra], axis=-1).astype(BF16)
        va_ref[h] = jnp.concatenate([v_ref[:, sl].astype(F32), v_extra], axis=-1).astype(BF16)


def fox_prep(proj, bf_pad, qg, kg, ts, shifted):
    b, s, _ = proj.shape
    nt = s // ts
    tri = jnp.tril(jnp.ones((ts, ts), F32)).astype(BF16)
    aug = jax.ShapeDtypeStruct((b, FOX_HEADS, s, LANES), BF16)
    edge = jax.ShapeDtypeStruct((b, nt, 1, LANES), F32)
    col = lambda c, w: pl.BlockSpec((None, ts, w), lambda bi, i: (bi, i, c // w))
    aug_spec = pl.BlockSpec((None, FOX_HEADS, ts, LANES), lambda bi, i: (bi, 0, i, 0))
    edge_spec = pl.BlockSpec((None, None, 1, LANES), lambda bi, i: (bi, i, 0, 0))
    const = lambda shape: pl.BlockSpec(shape, lambda bi, i: (0,) * len(shape))
    return pl.pallas_call(
        functools.partial(_fox_prep_kernel, shifted=shifted),
        grid=(b, nt),
        in_specs=[col(COL_FQ, FOX_W), col(COL_FK, FOX_W), col(COL_FV, FOX_W), col(COL_SMALL, LANES),
                  const((1, LANES)), const((1, FOX_HEAD_DIM)), const((1, FOX_HEAD_DIM)), const((ts, ts))],
        out_specs=[aug_spec, aug_spec, aug_spec, edge_spec, edge_spec],
        out_shape=[aug, aug, aug, edge, edge],
        scratch_shapes=[pltpu.VMEM((1, LANES), F32)],
        compiler_params=_params("parallel", "arbitrary"),
    )(proj, proj, proj, proj, bf_pad, qg, kg, tri)


def _fox_flash_kernel(jlo_ref, q_ref, k_ref, v_ref, o_ref, *, shifted):
    tq = q_ref.shape[0] // 2
    i = pl.program_id(2)
    flat = (pl.program_id(0) * pl.num_programs(1) + pl.program_id(1)) * pl.num_programs(2) + i
    rc = 2 * tq // FOX_CHAINS
    qs = [q_ref[c * rc:(c + 1) * rc, :] for c in range(FOX_CHAINS)]
    qpos = lax.broadcasted_iota(jnp.int32, (rc, tq), 0)
    kpos = lax.broadcasted_iota(jnp.int32, (rc, tq), 1)

    def sweep(j, carry, chains, key_offset=None):
        rows = pl.ds(pl.multiple_of(j * tq, tq), tq)
        k = k_ref[rows, :]
        v = v_ref[rows, :]
        logits = [_dot_nt(qs[c], k) for c in chains]
        carry = list(carry)
        for c, s in zip(chains, logits):
            if key_offset is not None and key_offset + tq > c * rc:
                s = jnp.where(kpos + key_offset <= qpos + c * rc, s, NEG_BIG)
            if shifted:
                carry[c] = carry[c] + _dot(jnp.exp2(s).astype(BF16), v)
            else:
                m, acc = carry[c]
                m_new = jnp.maximum(m, jnp.max(s, axis=-1, keepdims=True))
                p = jnp.exp2(s - m_new).astype(BF16)
                carry[c] = (m_new, jnp.exp2(m - m_new) * acc + _dot(p, v))
        return tuple(carry)

    every = tuple(range(FOX_CHAINS))
    acc0 = jnp.zeros((rc, LANES), F32)
    init = acc0 if shifted else (jnp.full((rc, 1), NEG_BIG, F32), acc0)
    carry = lax.fori_loop(jlo_ref[flat], 2 * i, lambda j, c: sweep(j, c, every), (init,) * FOX_CHAINS)
    carry = sweep(2 * i, carry, every, 0)
    carry = sweep(2 * i + 1, carry, every[FOX_CHAINS // 2:], tq)
    for c, state in enumerate(carry):
        acc = state if shifted else state[1]
        out = acc[:, :FOX_HEAD_DIM] / acc[:, FOX_HEAD_DIM:FOX_HEAD_DIM + 1]
        o_ref[c * rc:(c + 1) * rc, :] = out.astype(o_ref.dtype)


def fox_flash(q_aug, k_aug, v_aug, jlo_pair, tq, shifted):
    b, nh, s, _ = q_aug.shape
    full = pl.BlockSpec((None, None, s, LANES), lambda bi, h, i, jl: (bi, h, 0, 0))
    return pl.pallas_call(
        functools.partial(_fox_flash_kernel, shifted=shifted),
        grid_spec=pltpu.PrefetchScalarGridSpec(
            num_scalar_prefetch=1,
            grid=(b, nh, s // (2 * tq)),
            in_specs=[pl.BlockSpec((None, None, 2 * tq, LANES), lambda bi, h, i, jl: (bi, h, i, 0)), full, full],
            out_specs=pl.BlockSpec((None, None, 2 * tq, FOX_HEAD_DIM), lambda bi, h, i, jl: (bi, h, i, 0)),
        ),
        out_shape=jax.ShapeDtypeStruct((b, nh, s, FOX_HEAD_DIM), BF16),
        compiler_params=_params("parallel", "parallel", "arbitrary"),
    )(jlo_pair, q_aug, k_aug, v_aug)


def fox_first_block(c_first, c_last, qk_bound):
    cf = c_first[:, :, 0, :FOX_HEADS].transpose(0, 2, 1)
    cl = c_last[:, :, 0, :FOX_HEADS].transpose(0, 2, 1)
    nq = cf.shape[-1]
    gap = cf[..., :, None] - cl[..., None, :]
    j_idx = jnp.arange(nq)
    skippable = (qk_bound + gap < -FOX_SKIP_MARGIN) & (j_idx[None, :] < j_idx[:, None])
    first = jnp.sum(skippable, axis=-1).astype(jnp.int32)
    return jnp.minimum(first[..., 0::2], first[..., 1::2]).reshape(-1)


def fox_attention(proj, bf_pad, qg, kg, ts):
    qk_bound = 2.0 * FOX_HEAD_DIM ** 0.5 * jnp.max(jnp.abs(qg)) * jnp.max(jnp.abs(kg)) * (1.0 + 1e-3)

    def run(shifted):
        def branch(proj, bf_pad, qg, kg, qk_bound):
            q_aug, k_aug, v_aug, c_first, c_last = fox_prep(proj, bf_pad, qg, kg, ts, shifted)
            jlo = fox_first_block(c_first, c_last, qk_bound)
            return fox_flash(q_aug, k_aug, v_aug, jlo, ts, shifted)
        return branch

    return lax.cond(qk_bound * LOG2E <= FOX_SHIFT_LIMIT, run(True), run(False), proj, bf_pad, qg, kg, qk_bound)


def _conv_kernel(a_ref, g_ref, ap_ref, gp_ref, w_ref, b_ref, lg_ref, lb_ref, o_ref, u_ref):
    ts = a_ref.shape[0]

    def glu(a, g):
        g = g.astype(F32)
        return a.astype(F32) * (1.0 / (1.0 + jnp.exp(-g)))

    prev = glu(ap_ref[...], gp_ref[...])
    u_ref[0, 0:CONV_HALO, :] = jnp.where(pl.program_id(1) > 0, prev, 0.0)
    u_ref[0, CONV_HALO:CONV_HALO + ts, :] = glu(a_ref[...], g_ref[...])
    u_ref[0, CONV_HALO + ts:, :] = jnp.zeros((SUBLANES, CONV_W), F32)
    base = u_ref[0]
    for r in range(1, SUBLANES):
        u_ref[r] = pltpu.roll(base, base.shape[0] - r, axis=0)
    first = CONV_HALO - (CONV_TAPS - 1)
    y = jnp.zeros((ts, CONV_W), F32) + b_ref[...]
    for j in range(CONV_TAPS):
        r, start = (first + j) % SUBLANES, (first + j) // SUBLANES * SUBLANES
        y = y + w_ref[j:j + 1, :] * u_ref[r, start:start + ts, :]
    mu = jnp.mean(y, axis=-1, keepdims=True)
    yc = y - mu
    var = jnp.mean(yc * yc, axis=-1, keepdims=True)
    y = yc * lax.rsqrt(var + EPS) * lg_ref[...] + lb_ref[...]
    o_ref[...] = _silu(y).astype(o_ref.dtype)


def conformer_conv(proj, w_pad, b, ln_g, ln_b, ts):
    bsz, s, _ = proj.shape
    ratio = ts // CONV_HALO
    main = lambda c: pl.BlockSpec((None, ts, CONV_W), lambda bi, i: (bi, i, c // CONV_W))
    halo = lambda c: pl.BlockSpec((None, CONV_HALO, CONV_W),
                                  lambda bi, i: (bi, jnp.maximum(i * ratio - 1, 0), c // CONV_W))
    const = lambda shape: pl.BlockSpec(shape, lambda bi, i: (0, 0))
    return pl.pallas_call(
        _conv_kernel,
        grid=(bsz, s // ts),
        in_specs=[main(COL_CA), main(COL_CG), halo(COL_CA), halo(COL_CG),
                  const((CONV_HALO, CONV_W)), const((1, CONV_W)), const((1, CONV_W)), const((1, CONV_W))],
        out_specs=pl.BlockSpec((None, ts, CONV_W), lambda bi, i: (bi, i, 0)),
        out_shape=jax.ShapeDtypeStruct((bsz, s, CONV_W), BF16),
        scratch_shapes=[pltpu.VMEM((SUBLANES, CONV_HALO + ts + SUBLANES, CONV_W), F32)],
        compiler_params=_params("parallel", "parallel"),
    )(proj, proj, proj, proj, w_pad, b, ln_g, ln_b)


def _gla_kernel(q_ref, k_ref, v0_ref, v1_ref, v2_ref, v3_ref, r0_ref, r1_ref, r2_ref, r3_ref,
                sm_ref, w2_ref, gb_ref, og_ref, tri_ref, o_ref, state_ref, bcum_ref):
    ts = q_ref.shape[0]
    c_len = GLA_CHUNK
    v_refs = (v0_ref, v1_ref, v2_ref, v3_ref)
    r_refs = (r0_ref, r1_ref, r2_ref, r3_ref)

    @pl.when(pl.program_id(1) == 0)
    def _():
        state_ref[...] = jnp.zeros_like(state_ref)

    log_a = _log_sigmoid(_dot(sm_ref[...], w2_ref[...]) + gb_ref[...]) * (1.0 / GLA_GATE_TAU)
    hi, mid, lo = _split3(log_a)
    tri = tri_ref[...]
    bcum_ref[...] = _dot(tri, hi) + _dot(tri, mid) + _dot(tri, lo)

    rr = lax.broadcasted_iota(jnp.int32, (c_len, c_len), 0)
    cc = lax.broadcasted_iota(jnp.int32, (c_len, c_len), 1)
    causal = cc <= rr
    scale = GLA_DK ** -0.5

    def chunk(ci, _):
        rows = pl.ds(pl.multiple_of(ci * c_len, c_len), c_len)
        for h in range(GLA_HEADS):
            hs = slice(h * GLA_DK, (h + 1) * GLA_DK)
            bc = bcum_ref[rows, hs]
            b_mid = bc[c_len // 2:c_len // 2 + 1, :]
            b_last = bc[c_len - 1:c_len, :]
            q = q_ref[rows, hs].astype(F32) * scale
            k = k_ref[rows, hs].astype(F32)
            v = v_refs[h][rows, :]
            q_rel = (q * jnp.exp(bc - b_mid)).astype(BF16)
            k_rel = (k * jnp.exp(b_mid - bc)).astype(BF16)
            scores = jnp.where(causal, _dot_nt(q_rel, k_rel), 0.0)
            state = state_ref[h]
            o = _dot(scores.astype(BF16), v) + _dot((q * jnp.exp(bc)).astype(BF16), state.astype(BF16))
            k_out_t = jnp.transpose(k * jnp.exp(b_last - bc))
            decay = jnp.transpose(jnp.broadcast_to(jnp.exp(b_last), (8, GLA_DK)))[:, 0:1]
            state_ref[h] = decay * state + _dot(k_out_t.astype(BF16), v)
            on = _rms(o, og_ref[...])
            o_ref[rows, h * GLA_DV:(h + 1) * GLA_DV] = (on * _silu(r_refs[h][rows, :].astype(F32))).astype(o_ref.dtype)
        return 0

    lax.fori_loop(0, ts // c_len, chunk, 0)


def gla(proj, w2_pad, gb, og, ts):
    bsz, s, _ = proj.shape
    blk = jnp.arange(ts) // GLA_CHUNK
    tri = (jnp.tril(jnp.ones((ts, ts), F32)) * (blk[:, None] == blk[None, :])).astype(BF16)
    col = lambda c, w: pl.BlockSpec((None, ts, w), lambda bi, i: (bi, i, c // w))
    const = lambda shape: pl.BlockSpec(shape, lambda bi, i: (0, 0))
    v_specs = [col(COL_GV + h * GLA_DV, GLA_DV) for h in range(GLA_HEADS)]
    r_specs = [col(COL_GR + h * GLA_DV, GLA_DV) for h in range(GLA_HEADS)]
    return pl.pallas_call(
        _gla_kernel,
        grid=(bsz, s // ts),
        in_specs=[col(COL_GQ, GLA_KW), col(COL_GK, GLA_KW)] + v_specs + r_specs + [
            col(COL_SMALL, LANES), const((LANES, GLA_KW)), const((1, GLA_KW)), const((1, GLA_DV)), const((ts, ts))],
        out_specs=pl.BlockSpec((None, ts, GLA_VW), lambda bi, i: (bi, i, 0)),
        out_shape=jax.ShapeDtypeStruct((bsz, s, GLA_VW), BF16),
        scratch_shapes=[pltpu.VMEM((GLA_HEADS, GLA_DK, GLA_DV), F32), pltpu.VMEM((ts, GLA_KW), F32)],
        compiler_params=_params("parallel", "arbitrary"),
    )(*([proj] * 11), w2_pad, gb, og, tri)


def _out_proj_kernel(h_ref, f_ref, c_ref, g_ref, w_ref, o_ref):
    fox = jnp.concatenate([f_ref[hd] for hd in range(FOX_HEADS)], axis=-1)
    acc = _dot(fox, w_ref[0:FOX_W, :])
    acc = acc + _dot(c_ref[...], w_ref[FOX_W:FOX_W + CONV_W, :])
    acc = acc + _dot(g_ref[...], w_ref[FOX_W + CONV_W:, :])
    o_ref[...] = h_ref[...] + acc


def out_proj(h, fox, conv, gla_o, w, tm):
    t, d = h.shape
    nt = fox.shape[2] // tm
    row = lambda w_: pl.BlockSpec((tm, w_), lambda i: (i, 0))
    heads = pl.BlockSpec((None, FOX_HEADS, tm, FOX_HEAD_DIM), lambda i: (i // nt, 0, i % nt, 0))
    return pl.pallas_call(
        _out_proj_kernel,
        grid=(t // tm,),
        in_specs=[row(d), heads, row(CONV_W), row(GLA_VW), pl.BlockSpec(w.shape, lambda i: (0, 0))],
        out_specs=row(d),
        out_shape=jax.ShapeDtypeStruct((t, d), F32),
        compiler_params=_params("parallel"),
    )(h, fox, conv, gla_o, w)


def _mem_kv_kernel(mem_ref, mg_ref, wk_ref, wv_ref, kg_ref, k_ref, v_ref):
    mn = _rms(mem_ref[...], mg_ref[...]).astype(BF16)
    k = _dot(mn, wk_ref[...])
    v_ref[...] = _dot(mn, wv_ref[...]).astype(v_ref.dtype)
    for h in range(XA_HEADS):
        hs = slice(h * XA_HEAD_DIM, (h + 1) * XA_HEAD_DIM)
        k_ref[:, hs] = _rms(k[:, hs], kg_ref[...]).astype(k_ref.dtype)


def mem_kv(mem, mem_g, wk, wv, kg):
    bsz, m, d = mem.shape
    nl, _, xw = wk.shape
    out = jax.ShapeDtypeStruct((nl, bsz, m, xw), BF16)
    o_spec = pl.BlockSpec((None, None, m, xw), lambda l, bi: (l, bi, 0, 0))
    w_spec = pl.BlockSpec((None, d, xw), lambda l, bi: (l, 0, 0))
    return pl.pallas_call(
        _mem_kv_kernel,
        grid=(nl, bsz),
        in_specs=[pl.BlockSpec((None, m, d), lambda l, bi: (bi, 0, 0)),
                  pl.BlockSpec((1, d), lambda l, bi: (0, 0)), w_spec, w_spec,
                  pl.BlockSpec((None, 1, XA_HEAD_DIM), lambda l, bi: (l, 0, 0))],
        out_specs=[o_spec, o_spec],
        out_shape=[out, out],
        compiler_params=_params("parallel", "parallel"),
    )(mem, mem_g, wk, wv, kg)


def _xattn_kernel(h_ref, g_ref, wq_ref, k_ref, v_ref, qg_ref, wo_ref, o_ref):
    h = h_ref[...]
    q = _dot(_rms(h, g_ref[...]).astype(BF16), wq_ref[...])
    scale = XA_HEAD_DIM ** -0.5
    outs = []
    for hd in range(XA_HEADS):
        hs = slice(hd * XA_HEAD_DIM, (hd + 1) * XA_HEAD_DIM)
        qn = (_rms(q[:, hs], qg_ref[...]) * scale).astype(BF16)
        s = _dot_nt(qn, k_ref[:, hs])
        p = jnp.exp(s - jnp.max(s, axis=-1, keepdims=True))
        denom = jnp.sum(p, axis=-1, keepdims=True)
        outs.append((_dot(p.astype(BF16), v_ref[:, hs]) / denom).astype(BF16))
    o_ref[...] = h + _dot(jnp.concatenate(outs, axis=-1), wo_ref[...])


def cross_attention(h, g, wq, k, v, qg, wo, tm):
    bsz, s, d = h.shape
    m, xw = k.shape[-2:]
    row = pl.BlockSpec((None, tm, d), lambda bi, i: (bi, i, 0))
    const = lambda shape: pl.BlockSpec(shape, lambda bi, i: (0, 0))
    kv = pl.BlockSpec((None, m, xw), lambda bi, i: (bi, 0, 0))
    return pl.pallas_call(
        _xattn_kernel,
        grid=(bsz, s // tm),
        in_specs=[row, const((1, d)), const((d, xw)), kv, kv, const((1, XA_HEAD_DIM)), const((xw, d))],
        out_specs=row,
        out_shape=jax.ShapeDtypeStruct((bsz, s, d), F32),
        compiler_params=_params("parallel", "parallel"),
    )(h, g, wq, k, v, qg, wo)


FFN_OUT_CHUNK = 512


def _ffn_body(xb_ref, wg_ref, wu_ref, wd_ref, acc_ref):
    xb = xb_ref[...]
    a = (_silu(_dot(xb, wg_ref[...])) * _dot(xb, wu_ref[...])).astype(BF16)
    for c in range(0, acc_ref.shape[-1], FFN_OUT_CHUNK):
        acc_ref[:, c:c + FFN_OUT_CHUNK] += _dot(a, wd_ref[:, c:c + FFN_OUT_CHUNK])


def _dense_ffn_kernel(h_ref, g_ref, wg_ref, wu_ref, wd_ref, o_ref, xb_ref):
    @pl.when(pl.program_id(1) == 0)
    def _():
        xb_ref[...] = _rms(h_ref[...], g_ref[...]).astype(BF16)
        o_ref[...] = h_ref[...]

    _ffn_body(xb_ref, wg_ref, wu_ref, wd_ref, o_ref)


def dense_ffn(h, g, wg, wu, wd, tm, tf):
    t, d = h.shape
    f = wg.shape[1]
    row = pl.BlockSpec((tm, d), lambda i, j: (i, 0))
    return pl.pallas_call(
        _dense_ffn_kernel,
        grid=(t // tm, f // tf),
        in_specs=[row, pl.BlockSpec((1, d), lambda i, j: (0, 0)),
                  pl.BlockSpec((d, tf), lambda i, j: (0, j)), pl.BlockSpec((d, tf), lambda i, j: (0, j)),
                  pl.BlockSpec((tf, d), lambda i, j: (j, 0))],
        out_specs=pl.BlockSpec((tm, d), lambda i, j: (i, 0), pipeline_mode=pl.Buffered(1)),
        out_shape=jax.ShapeDtypeStruct((t, d), F32),
        scratch_shapes=[pltpu.VMEM((tm, d), BF16)],
        compiler_params=_params("parallel", "arbitrary"),
    )(h, g, wg, wu, wd)


def _expert_ffn_kernel(te_ref, nu_ref, x_ref, wg_ref, wu_ref, wd_ref, o_ref, xb_ref):
    del te_ref

    @pl.when(pl.program_id(1) == 0)
    def _():
        xb_ref[...] = x_ref[...].astype(BF16)
        o_ref[...] = jnp.zeros_like(o_ref)

    @pl.when(pl.program_id(0) < nu_ref[0])
    def _():
        _ffn_body(xb_ref, wg_ref, wu_ref, wd_ref, o_ref)


def expert_ffn(xs, tile_expert, n_used, wg, wu, wd, tm, tf):
    r, d = xs.shape
    f = wg.shape[-1]
    nj = f // tf
    ti = lambda i, nu: jnp.minimum(i, nu[0] - 1)
    tj = lambda i, j, nu: jnp.where(i < nu[0], j, nj - 1)
    row = pl.BlockSpec((tm, d), lambda i, j, te, nu: (ti(i, nu), 0))
    w_in = pl.BlockSpec((None, d, tf), lambda i, j, te, nu: (te[ti(i, nu)], 0, tj(i, j, nu)))
    w_dn = pl.BlockSpec((None, tf, d), lambda i, j, te, nu: (te[ti(i, nu)], tj(i, j, nu), 0))
    return pl.pallas_call(
        _expert_ffn_kernel,
        grid_spec=pltpu.PrefetchScalarGridSpec(
            num_scalar_prefetch=2,
            grid=(r // tm, nj),
            in_specs=[row, w_in, w_in, w_dn],
            out_specs=pl.BlockSpec((tm, d), lambda i, j, te, nu: (i, 0), pipeline_mode=pl.Buffered(1)),
            scratch_shapes=[pltpu.VMEM((tm, d), BF16)],
        ),
        out_shape=jax.ShapeDtypeStruct((r, d), F32),
        compiler_params=_params("arbitrary", "arbitrary"),
    )(tile_expert, n_used, xs, wg, wu, wd)


def _router_kernel(h_ref, g_ref, rw_ref, rb_ref, tri_ref, hn_ref, info_ref, cnt_ref, carry_ref):
    @pl.when(pl.program_id(0) == 0)
    def _():
        carry_ref[...] = jnp.zeros_like(carry_ref)

    hn = _rms(h_ref[...], g_ref[...])
    hn_ref[...] = hn
    tm = hn.shape[0]
    x3 = _split3(hn)
    w3 = _split3(rw_ref[...])
    logits = rb_ref[...]
    for a, bpart in ((2, 0), (1, 1), (0, 2), (1, 0), (0, 1), (0, 0)):
        logits = logits + _dot(x3[a], w3[bpart])
    lane = lax.broadcasted_iota(jnp.int32, (tm, LANES), 1).astype(F32)
    logits = jnp.where(lane < N_EXPERTS, logits, NEG_BIG)
    v_a = jnp.max(logits, axis=-1, keepdims=True)
    e_a = jnp.min(jnp.where(logits == v_a, lane, float(LANES)), axis=-1, keepdims=True)
    rest = jnp.where(lane == e_a, NEG_BIG, logits)
    v_b = jnp.max(rest, axis=-1, keepdims=True)
    e_b = jnp.min(jnp.where(rest == v_b, lane, float(LANES)), axis=-1, keepdims=True)
    t = jnp.exp(v_b - v_a)
    w_a = 1.0 / (1.0 + t)
    w_b = t * w_a
    chosen = (lane == e_a) | (lane == e_b)
    onehot = jnp.where(chosen, 1.0, 0.0)
    before = _dot(tri_ref[...], onehot.astype(BF16)) + carry_ref[...]
    carry_ref[...] += jnp.sum(onehot, axis=0, keepdims=True)
    cnt_ref[...] = carry_ref[...]
    rank_a = jnp.sum(jnp.where(lane == e_a, before, 0.0), axis=-1, keepdims=True)
    rank_b = jnp.sum(jnp.where(lane == e_b, before, 0.0), axis=-1, keepdims=True)
    info_ref[...] = jnp.where(lane == 0, e_a, jnp.where(lane == 1, e_b, jnp.where(
        lane == 2, w_a, jnp.where(lane == 3, w_b, jnp.where(lane == 4, rank_a, jnp.where(lane == 5, rank_b, 0.0))))))


def router(h, g, rw_pad, rb_pad, tm):
    t, d = h.shape
    tri = jnp.tril(jnp.ones((tm, tm), F32), -1).astype(BF16)
    row = lambda w: pl.BlockSpec((tm, w), lambda i: (i, 0))
    const = lambda shape: pl.BlockSpec(shape, lambda i: (0, 0))
    return pl.pallas_call(
        _router_kernel,
        grid=(t // tm,),
        in_specs=[row(d), const((1, d)), const((d, LANES)), const((1, LANES)), const((tm, tm))],
        out_specs=[row(d), row(LANES), const((1, LANES))],
        out_shape=[jax.ShapeDtypeStruct((t, d), F32), jax.ShapeDtypeStruct((t, LANES), F32),
                   jax.ShapeDtypeStruct((1, LANES), F32)],
        scratch_shapes=[pltpu.VMEM((1, LANES), F32)],
        compiler_params=_params("arbitrary"),
    )(h, g, rw_pad, rb_pad, tri)


def _row_copy(src_ref, src_row, dst_ref, dst_row, sem):
    return pltpu.make_async_copy(src_ref.at[pl.ds(src_row, 1), :], dst_ref.at[pl.ds(dst_row, 1), :], sem)


def _dispatch_kernel(da_ref, db_ref, x_ref, init_ref, xs_ref, sem):
    del init_ref
    tm = x_ref.shape[0]
    base = pl.program_id(0) * tm

    def issue(r, _):
        _row_copy(x_ref, r, xs_ref, da_ref[base + r], sem).start()
        _row_copy(x_ref, r, xs_ref, db_ref[base + r], sem).start()
        return 0

    def drain(r, _):
        _row_copy(x_ref, 0, xs_ref, 0, sem).wait()
        _row_copy(x_ref, 0, xs_ref, 0, sem).wait()
        return 0

    lax.fori_loop(0, tm, issue, 0, unroll=8)
    lax.fori_loop(0, tm, drain, 0, unroll=8)


def dispatch(x, dest_a, dest_b, n_rows, tm):
    t, d = x.shape
    return pl.pallas_call(
        _dispatch_kernel,
        grid_spec=pltpu.PrefetchScalarGridSpec(
            num_scalar_prefetch=2,
            grid=(t // tm,),
            in_specs=[pl.BlockSpec((tm, d), lambda i, da, db: (i, 0)), pl.BlockSpec(memory_space=pl.ANY)],
            out_specs=pl.BlockSpec(memory_space=pl.ANY),
            scratch_shapes=[pltpu.SemaphoreType.DMA(())],
        ),
        out_shape=jax.ShapeDtypeStruct((n_rows, d), x.dtype),
        input_output_aliases={3: 0},
        compiler_params=_params("arbitrary"),
    )(dest_a, dest_b, x, jnp.zeros((n_rows, d), x.dtype))


def _combine_kernel(da_ref, db_ref, h_ref, info_ref, ys_ref, o_ref, ya_ref, yb_ref, sem):
    tm = h_ref.shape[0]
    base = pl.program_id(0) * tm

    def issue(r, _):
        _row_copy(ys_ref, da_ref[base + r], ya_ref, r, sem).start()
        _row_copy(ys_ref, db_ref[base + r], yb_ref, r, sem).start()
        return 0

    def drain(r, _):
        _row_copy(ys_ref, 0, ya_ref, 0, sem).wait()
        _row_copy(ys_ref, 0, yb_ref, 0, sem).wait()
        return 0

    lax.fori_loop(0, tm, issue, 0, unroll=8)
    lax.fori_loop(0, tm, drain, 0, unroll=8)
    info = info_ref[...]
    o_ref[...] = h_ref[...] + info[:, 2:3] * ya_ref[...] + info[:, 3:4] * yb_ref[...]


def combine(h, info, ys, dest_a, dest_b, tm):
    t, d = h.shape
    row = lambda w: pl.BlockSpec((tm, w), lambda i, da, db: (i, 0))
    return pl.pallas_call(
        _combine_kernel,
        grid_spec=pltpu.PrefetchScalarGridSpec(
            num_scalar_prefetch=2,
            grid=(t // tm,),
            in_specs=[row(d), row(LANES), pl.BlockSpec(memory_space=pl.ANY)],
            out_specs=row(d),
            scratch_shapes=[pltpu.VMEM((tm, d), F32), pltpu.VMEM((tm, d), F32), pltpu.SemaphoreType.DMA(())],
        ),
        out_shape=jax.ShapeDtypeStruct((t, d), F32),
        compiler_params=_params("arbitrary"),
    )(dest_a, dest_b, h, info, ys)


def moe_ffn(h, g, rw, rb, wg, wu, wd, tm_route, tm_rows, tf, tm_move):
    t, d = h.shape
    rw_pad = jnp.zeros((d, LANES), F32).at[:, :N_EXPERTS].set(rw)
    rb_pad = jnp.zeros((1, LANES), F32).at[0, :N_EXPERTS].set(rb)
    hn, info, counts = router(h, g, rw_pad, rb_pad, tm_route)
    counts = counts[0, :N_EXPERTS].astype(jnp.int32)
    padded = (counts + tm_rows - 1) // tm_rows * tm_rows
    ends = jnp.cumsum(padded)
    starts = ends - padded
    e_a, e_b = info[:, 0].astype(jnp.int32), info[:, 1].astype(jnp.int32)
    dest_a = starts[e_a] + info[:, 4].astype(jnp.int32)
    dest_b = starts[e_b] + info[:, 5].astype(jnp.int32)
    n_rows = 2 * t + N_EXPERTS * tm_rows
    n_tiles = n_rows // tm_rows
    n_used = (ends[-1] // tm_rows).astype(jnp.int32).reshape(1)
    tile_start = jnp.arange(n_tiles, dtype=jnp.int32) * tm_rows
    tile_expert = jnp.minimum(jnp.sum(tile_start[:, None] >= ends[None, :], axis=-1), N_EXPERTS - 1).astype(jnp.int32)
    xs = dispatch(hn, dest_a, dest_b, n_rows, tm_move)
    ys = expert_ffn(xs, tile_expert, n_used, wg, wu, wd, tm_rows, tf)
    return combine(h, info, ys, dest_a, dest_b, tm_move)


def _pack_w_in(w):
    d = w.shape[0]
    pts = [0, 512, 1024, 1536, 1544, 2056, 2568, 3080, 3592, 4616, 4632, 5656]
    fq, fk, fv, ff, ca, cg, gq, gk, gv, glr, gr = [w[:, a:b] for a, b in zip(pts[:-1], pts[1:])]
    pad = jnp.zeros((d, LANES - FOX_HEADS - GLA_GATE_RANK), w.dtype)
    return jnp.concatenate([fq, fk, fv, ca, cg, gq, gk, gv, gr, ff, glr, pad], axis=1).astype(BF16)


def _tile(n, pref):
    return min(n, pref)


def kernel(x, mem, mem_norm, mix_norm, w_in, fox_forget_bias, fox_q_norm, fox_k_norm, conv_w, conv_b,
           conv_ln_g, conv_ln_b, gla_gate_w2, gla_gate_b, gla_out_norm, w_out, xa_norm, xa_wq, xa_wk, xa_wv,
           xa_q_norm, xa_k_norm, xa_wo, ffn_norm, dense_wg, dense_wu, dense_wd, router_w, router_b,
           moe_wg, moe_wu, moe_wd):
    bsz, s, d = x.shape
    t = bsz * s
    depth = mix_norm.shape[0]
    ts = _tile(s, 512)
    row = lambda v: v.reshape(1, -1).astype(F32)

    k_mem, v_mem = mem_kv(mem, row(mem_norm), xa_wk.astype(BF16), xa_wv.astype(BF16),
                          xa_k_norm.reshape(depth, 1, XA_HEAD_DIM))
    h = x.reshape(t, d)
    for l in range(depth):
        proj = norm_matmul(h, row(mix_norm[l]), _pack_w_in(w_in[l]), _tile(t, 1024), PACK_W // 3)
        proj = proj.reshape(bsz, s, PACK_W)
        bf_pad = jnp.zeros((1, LANES), F32).at[0, :FOX_HEADS].set(fox_forget_bias[l])
        qg, kg = row(fox_q_norm[l]), row(fox_k_norm[l])
        fox = fox_attention(proj, bf_pad, qg, kg, ts)
        conv_w_pad = jnp.zeros((CONV_HALO, CONV_W), F32).at[:CONV_TAPS].set(conv_w[l])
        conv = conformer_conv(proj, conv_w_pad, row(conv_b[l]), row(conv_ln_g[l]), row(conv_ln_b[l]), ts)
        w2_pad = jnp.zeros((LANES, GLA_KW), F32).at[FOX_HEADS:FOX_HEADS + GLA_GATE_RANK].set(gla_gate_w2[l])
        gla_o = gla(proj, w2_pad.astype(BF16), row(gla_gate_b[l]), row(gla_out_norm[l]), ts)
        h = out_proj(h, fox, conv.reshape(t, CONV_W), gla_o.reshape(t, GLA_VW), w_out[l].astype(BF16),
                     _tile(t, 512))
        h = cross_attention(h.reshape(bsz, s, d), row(xa_norm[l]), xa_wq[l].astype(BF16), k_mem[l], v_mem[l],
                            row(xa_q_norm[l]), xa_wo[l].astype(BF16), ts).reshape(t, d)
        j = l // 2
        if l % 2 == 0:
            h = dense_ffn(h, row(ffn_norm[l]), dense_wg[j].astype(BF16), dense_wu[j].astype(BF16),
                          dense_wd[j].astype(BF16), _tile(t, 1024), 512)
        else:
            h = moe_ffn(h, row(ffn_norm[l]), router_w[j], router_b[j], moe_wg[j].astype(BF16),
                        moe_wu[j].astype(BF16), moe_wd[j].astype(BF16), _tile(t, 512), _tile(t, 1024), 512,
                        _tile(t, 256))
    return h.reshape(bsz, s, d)
```

```python
import functools

import jax
import jax.numpy as jnp
from jax import lax
from jax.experimental import pallas as pl
from jax.experimental.pallas import tpu as pltpu

F32 = jnp.float32
BF16 = jnp.bfloat16
EPS = 1e-6

FOX_HEADS = 8
FOX_HEAD_DIM = 64
CONV_TAPS = 31
CONV_HALO = 32
GLA_HEADS = 4
GLA_DK = 128
GLA_DV = 256
GLA_GATE_RANK = 16
GLA_GATE_TAU = 16.0
GLA_CHUNK = 64
XA_HEADS = 4
XA_HEAD_DIM = 128
N_EXPERTS = 8
LANES = 128
SUBLANES = 8

FOX_W = FOX_HEADS * FOX_HEAD_DIM
CONV_W = 512
GLA_KW = GLA_HEADS * GLA_DK
GLA_VW = GLA_HEADS * GLA_DV
COL_FQ, COL_FK, COL_FV = 0, 512, 1024
COL_CA, COL_CG = 1536, 2048
COL_GQ, COL_GK, COL_GV, COL_GR = 2560, 3072, 3584, 4608
COL_SMALL = 5632
PACK_W = COL_SMALL + LANES

FOX_SKIP_MARGIN = 110.0
NEG_BIG = -1e30
FOX_CHAINS = 4
FOX_RANGE_LIMIT = 100.0

VMEM_LIMIT = 56 * 1024 * 1024


def _params(*sem):
    return pltpu.CompilerParams(dimension_semantics=sem, vmem_limit_bytes=VMEM_LIMIT)


def _dot(a, b):
    return jnp.dot(a, b, preferred_element_type=F32)


def _dot_nt(a, b):
    return lax.dot_general(a, b, (((1,), (1,)), ((), ())), preferred_element_type=F32)


def _split3(x):
    hi = x.astype(BF16)
    r = x - hi.astype(F32)
    mid = r.astype(BF16)
    lo = (r - mid.astype(F32)).astype(BF16)
    return hi, mid, lo


def _log_sigmoid(z):
    return jnp.minimum(z, 0.0) - jnp.log1p(jnp.exp(-jnp.abs(z)))


def _silu(z):
    return z * (1.0 / (1.0 + jnp.exp(-z)))


def _rms(x, g):
    ms = jnp.mean(x * x, axis=-1, keepdims=True)
    return x * lax.rsqrt(ms + EPS) * g


def _norm_mm_kernel(h_ref, g_ref, w_ref, o_ref, hn_ref):
    @pl.when(pl.program_id(1) == 0)
    def _():
        hn_ref[...] = _rms(h_ref[...], g_ref[...]).astype(BF16)

    o_ref[...] = _dot(hn_ref[...], w_ref[...]).astype(o_ref.dtype)


def norm_matmul(h, g, w, tm, tn):
    t, d = h.shape
    n = w.shape[1]
    return pl.pallas_call(
        _norm_mm_kernel,
        grid=(t // tm, n // tn),
        in_specs=[pl.BlockSpec((tm, d), lambda i, j: (i, 0)),
                  pl.BlockSpec((1, d), lambda i, j: (0, 0)),
                  pl.BlockSpec((d, tn), lambda i, j: (0, j))],
        out_specs=pl.BlockSpec((tm, tn), lambda i, j: (i, j)),
        out_shape=jax.ShapeDtypeStruct((t, n), BF16),
        scratch_shapes=[pltpu.VMEM((tm, d), BF16)],
        compiler_params=_params("parallel", "arbitrary"),
    )(h, g, w)


LOG2E = 1.4426950408889634


def _fox_prep_kernel(q_ref, k_ref, v_ref, sm_ref, bf_ref, qg_ref, kg_ref, tri_ref,
                     qa_ref, ka_ref, va_ref, cf_ref, cl_ref, carry_ref):
    ts = q_ref.shape[0]

    @pl.when(pl.program_id(1) == 0)
    def _():
        carry_ref[...] = jnp.zeros_like(carry_ref)

    lf = _log_sigmoid(sm_ref[...].astype(F32) + bf_ref[...])
    hi, mid, lo = _split3(lf)
    tri = tri_ref[...]
    cum = _dot(tri, hi) + _dot(tri, mid) + _dot(tri, lo) + carry_ref[...]
    carry_ref[...] = cum[ts - 1:ts, :]
    cf_ref[...] = cum[0:1, :]
    cl_ref[...] = cum[ts - 1:ts, :]

    lane = lax.broadcasted_iota(jnp.int32, (ts, FOX_HEAD_DIM), 1)
    scale = FOX_HEAD_DIM ** -0.5 * LOG2E
    cum2 = cum * LOG2E
    for h in range(FOX_HEADS):
        sl = slice(h * FOX_HEAD_DIM, (h + 1) * FOX_HEAD_DIM)
        c = cum2[:, h:h + 1]
        c_hi = c.astype(BF16).astype(F32)
        r = c - c_hi
        c_mid = r.astype(BF16).astype(F32)
        c_lo = r - c_mid
        qn = _rms(q_ref[:, sl].astype(F32), qg_ref[...]) * scale
        kn = _rms(k_ref[:, sl].astype(F32), kg_ref[...])
        q_extra = jnp.where(lane == 0, c_hi, jnp.where(lane == 1, c_mid, jnp.where(
            lane == 2, c_lo, jnp.where(lane < 6, 1.0, 0.0))))
        k_extra = jnp.where(lane < 3, 1.0, jnp.where(lane == 3, -c_hi, jnp.where(
            lane == 4, -c_mid, jnp.where(lane == 5, -c_lo, 0.0))))
        v_extra = jnp.where(lane == 0, 1.0, 0.0)
        qa_ref[h] = jnp.concatenate([qn, q_extra], axis=-1).astype(BF16)
        ka_ref[h] = jnp.concatenate([kn, k_extra], axis=-1).astype(BF16)
        va_ref[h] = jnp.concatenate([v_ref[:, sl].astype(F32), v_extra], axis=-1).astype(BF16)


def fox_prep(proj, bf_pad, qg, kg, ts):
    b, s, _ = proj.shape
    nt = s // ts
    tri = jnp.tril(jnp.ones((ts, ts), F32)).astype(BF16)
    aug = jax.ShapeDtypeStruct((b, FOX_HEADS, s, LANES), BF16)
    edge = jax.ShapeDtypeStruct((b, nt, 1, LANES), F32)
    col = lambda c, w: pl.BlockSpec((None, ts, w), lambda bi, i: (bi, i, c // w))
    aug_spec = pl.BlockSpec((None, FOX_HEADS, ts, LANES), lambda bi, i: (bi, 0, i, 0))
    edge_spec = pl.BlockSpec((None, None, 1, LANES), lambda bi, i: (bi, i, 0, 0))
    const = lambda shape: pl.BlockSpec(shape, lambda bi, i: (0,) * len(shape))
    return pl.pallas_call(
        _fox_prep_kernel,
        grid=(b, nt),
        in_specs=[col(COL_FQ, FOX_W), col(COL_FK, FOX_W), col(COL_FV, FOX_W), col(COL_SMALL, LANES),
                  const((1, LANES)), const((1, FOX_HEAD_DIM)), const((1, FOX_HEAD_DIM)), const((ts, ts))],
        out_specs=[aug_spec, aug_spec, aug_spec, edge_spec, edge_spec],
        out_shape=[aug, aug, aug, edge, edge],
        scratch_shapes=[pltpu.VMEM((1, LANES), F32)],
        compiler_params=_params("parallel", "arbitrary"),
    )(proj, proj, proj, proj, bf_pad, qg, kg, tri)


def _fox_flash_kernel(jlo_ref, q_ref, k_ref, v_ref, o_ref, *, bounded):
    tq = q_ref.shape[0] // 2
    i = pl.program_id(2)
    flat = (pl.program_id(0) * pl.num_programs(1) + pl.program_id(1)) * pl.num_programs(2) + i
    rc = 2 * tq // FOX_CHAINS
    qs = [q_ref[c * rc:(c + 1) * rc, :] for c in range(FOX_CHAINS)]
    qpos = lax.broadcasted_iota(jnp.int32, (rc, tq), 0)
    kpos = lax.broadcasted_iota(jnp.int32, (rc, tq), 1)

    def sweep(j, carry, chains, key_offset=None):
        rows = pl.ds(pl.multiple_of(j * tq, tq), tq)
        k = k_ref[rows, :]
        v = v_ref[rows, :]
        logits = [_dot_nt(qs[c], k) for c in chains]
        carry = list(carry)
        for c, s in zip(chains, logits):
            if key_offset is not None and key_offset + tq > c * rc:
                s = jnp.where(kpos + key_offset <= qpos + c * rc, s, NEG_BIG)
            if bounded:
                carry[c] = carry[c] + _dot(jnp.exp2(s).astype(BF16), v)
            else:
                m, acc = carry[c]
                m_new = jnp.maximum(m, jnp.max(s, axis=-1, keepdims=True))
                p = jnp.exp2(s - m_new).astype(BF16)
                carry[c] = (m_new, jnp.exp2(m - m_new) * acc + _dot(p, v))
        return tuple(carry)

    every = tuple(range(FOX_CHAINS))
    acc0 = jnp.zeros((rc, LANES), F32)
    init = acc0 if bounded else (jnp.full((rc, 1), NEG_BIG, F32), acc0)
    carry = lax.fori_loop(jlo_ref[flat], 2 * i, lambda j, c: sweep(j, c, every), (init,) * FOX_CHAINS)
    carry = sweep(2 * i, carry, every, 0)
    carry = sweep(2 * i + 1, carry, every[FOX_CHAINS // 2:], tq)
    for c, state in enumerate(carry):
        acc = state if bounded else state[1]
        out = acc[:, :FOX_HEAD_DIM] / acc[:, FOX_HEAD_DIM:FOX_HEAD_DIM + 1]
        o_ref[c * rc:(c + 1) * rc, :] = out.astype(o_ref.dtype)


def fox_flash(q_aug, k_aug, v_aug, jlo_pair, tq, bounded):
    b, nh, s, _ = q_aug.shape
    full = pl.BlockSpec((None, None, s, LANES), lambda bi, h, i, jl: (bi, h, 0, 0))
    return pl.pallas_call(
        functools.partial(_fox_flash_kernel, bounded=bounded),
        grid_spec=pltpu.PrefetchScalarGridSpec(
            num_scalar_prefetch=1,
            grid=(b, nh, s // (2 * tq)),
            in_specs=[pl.BlockSpec((None, None, 2 * tq, LANES), lambda bi, h, i, jl: (bi, h, i, 0)), full, full],
            out_specs=pl.BlockSpec((None, None, 2 * tq, FOX_HEAD_DIM), lambda bi, h, i, jl: (bi, h, i, 0)),
        ),
        out_shape=jax.ShapeDtypeStruct((b, nh, s, FOX_HEAD_DIM), BF16),
        compiler_params=_params("parallel", "parallel", "arbitrary"),
    )(jlo_pair, q_aug, k_aug, v_aug)


def fox_first_block(c_first, c_last, qk_bound):
    cf = c_first[:, :, 0, :FOX_HEADS].transpose(0, 2, 1)
    cl = c_last[:, :, 0, :FOX_HEADS].transpose(0, 2, 1)
    nq = cf.shape[-1]
    gap = cf[..., :, None] - cl[..., None, :]
    j_idx = jnp.arange(nq)
    skippable = (qk_bound + gap < -FOX_SKIP_MARGIN) & (j_idx[None, :] < j_idx[:, None])
    first = jnp.sum(skippable, axis=-1).astype(jnp.int32)
    return jnp.minimum(first[..., 0::2], first[..., 1::2]).reshape(-1)


def fox_attention(proj, bf_pad, qg, kg, ts):
    qk_bound = 2.0 * FOX_HEAD_DIM ** 0.5 * jnp.max(jnp.abs(qg)) * jnp.max(jnp.abs(kg)) * (1.0 + 1e-3)
    q_aug, k_aug, v_aug, c_first, c_last = fox_prep(proj, bf_pad, qg, kg, ts)
    jlo = fox_first_block(c_first, c_last, qk_bound)
    sweep = lambda bounded: functools.partial(fox_flash, tq=ts, bounded=bounded)
    return lax.cond(qk_bound * LOG2E <= FOX_RANGE_LIMIT, sweep(True), sweep(False), q_aug, k_aug, v_aug, jlo)


def _conv_kernel(a_ref, g_ref, ap_ref, gp_ref, w_ref, b_ref, lg_ref, lb_ref, o_ref, u_ref):
    ts = a_ref.shape[0]

    def glu(a, g):
        g = g.astype(F32)
        return a.astype(F32) * (1.0 / (1.0 + jnp.exp(-g)))

    prev = glu(ap_ref[...], gp_ref[...])
    u_ref[0, 0:CONV_HALO, :] = jnp.where(pl.program_id(1) > 0, prev, 0.0)
    u_ref[0, CONV_HALO:CONV_HALO + ts, :] = glu(a_ref[...], g_ref[...])
    u_ref[0, CONV_HALO + ts:, :] = jnp.zeros((SUBLANES, CONV_W), F32)
    base = u_ref[0]
    for r in range(1, SUBLANES):
        u_ref[r] = pltpu.roll(base, base.shape[0] - r, axis=0)
    first = CONV_HALO - (CONV_TAPS - 1)
    y = jnp.zeros((ts, CONV_W), F32) + b_ref[...]
    for j in range(CONV_TAPS):
        r, start = (first + j) % SUBLANES, (first + j) // SUBLANES * SUBLANES
        y = y + w_ref[j:j + 1, :] * u_ref[r, start:start + ts, :]
    mu = jnp.mean(y, axis=-1, keepdims=True)
    yc = y - mu
    var = jnp.mean(yc * yc, axis=-1, keepdims=True)
    y = yc * lax.rsqrt(var + EPS) * lg_ref[...] + lb_ref[...]
    o_ref[...] = _silu(y).astype(o_ref.dtype)


def conformer_conv(proj, w_pad, b, ln_g, ln_b, ts):
    bsz, s, _ = proj.shape
    ratio = ts // CONV_HALO
    main = lambda c: pl.BlockSpec((None, ts, CONV_W), lambda bi, i: (bi, i, c // CONV_W))
    halo = lambda c: pl.BlockSpec((None, CONV_HALO, CONV_W),
                                  lambda bi, i: (bi, jnp.maximum(i * ratio - 1, 0), c // CONV_W))
    const = lambda shape: pl.BlockSpec(shape, lambda bi, i: (0, 0))
    return pl.pallas_call(
        _conv_kernel,
        grid=(bsz, s // ts),
        in_specs=[main(COL_CA), main(COL_CG), halo(COL_CA), halo(COL_CG),
                  const((CONV_HALO, CONV_W)), const((1, CONV_W)), const((1, CONV_W)), const((1, CONV_W))],
        out_specs=pl.BlockSpec((None, ts, CONV_W), lambda bi, i: (bi, i, 0)),
        out_shape=jax.ShapeDtypeStruct((bsz, s, CONV_W), BF16),
        scratch_shapes=[pltpu.VMEM((SUBLANES, CONV_HALO + ts + SUBLANES, CONV_W), F32)],
        compiler_params=_params("parallel", "parallel"),
    )(proj, proj, proj, proj, w_pad, b, ln_g, ln_b)


def _gla_kernel(q_ref, k_ref, v0_ref, v1_ref, v2_ref, v3_ref, r0_ref, r1_ref, r2_ref, r3_ref,
                sm_ref, w2_ref, gb_ref, og_ref, tri_ref, o_ref, state_ref, bcum_ref):
    bsz, ts = q_ref.shape[0], q_ref.shape[1]
    c_len = GLA_CHUNK
    v_refs = (v0_ref, v1_ref, v2_ref, v3_ref)
    r_refs = (r0_ref, r1_ref, r2_ref, r3_ref)

    @pl.when(pl.program_id(0) == 0)
    def _():
        state_ref[...] = jnp.zeros_like(state_ref)

    tri = tri_ref[...]
    for b in range(bsz):
        log_a = _log_sigmoid(_dot(sm_ref[b], w2_ref[...]) + gb_ref[...]) * (1.0 / GLA_GATE_TAU)
        hi, mid, lo = _split3(log_a)
        bcum_ref[b] = _dot(tri, hi) + _dot(tri, mid) + _dot(tri, lo)

    rr = lax.broadcasted_iota(jnp.int32, (c_len, c_len), 0)
    cc = lax.broadcasted_iota(jnp.int32, (c_len, c_len), 1)
    causal = cc <= rr
    scale = GLA_DK ** -0.5
    chains = [(b, h) for b in range(bsz) for h in range(GLA_HEADS)]

    def chunk(ci, _):
        rows = pl.ds(pl.multiple_of(ci * c_len, c_len), c_len)
        prep = []
        for b, h in chains:
            hs = slice(h * GLA_DK, (h + 1) * GLA_DK)
            bc = bcum_ref[b, rows, hs]
            b_mid = bc[c_len // 2:c_len // 2 + 1, :]
            b_last = bc[c_len - 1:c_len, :]
            q = q_ref[b, rows, hs].astype(F32) * scale
            k = k_ref[b, rows, hs].astype(F32)
            q_rel = (q * jnp.exp(bc - b_mid)).astype(BF16)
            k_rel = (k * jnp.exp(b_mid - bc)).astype(BF16)
            q_in = (q * jnp.exp(bc)).astype(BF16)
            k_out_t = jnp.transpose(k * jnp.exp(b_last - bc)).astype(BF16)
            decay = jnp.transpose(jnp.broadcast_to(jnp.exp(b_last), (8, GLA_DK)))[:, 0:1]
            prep.append((q_rel, k_rel, q_in, k_out_t, decay))
        scores = [jnp.where(causal, _dot_nt(p[0], p[1]), 0.0).astype(BF16) for p in prep]
        states = [state_ref[b * GLA_HEADS + h] for b, h in chains]
        inter = [_dot(p[2], st.astype(BF16)) for p, st in zip(prep, states)]
        vals = [v_refs[h][b, rows, :] for b, h in chains]
        outs = [_dot(sc, v) + it for sc, v, it in zip(scores, vals, inter)]
        for (b, h), p, st, v in zip(chains, prep, states, vals):
            state_ref[b * GLA_HEADS + h] = p[4] * st + _dot(p[3], v)
        for (b, h), o in zip(chains, outs):
            gate = _silu(r_refs[h][b, rows, :].astype(F32))
            o_ref[b, rows, h * GLA_DV:(h + 1) * GLA_DV] = (_rms(o, og_ref[...]) * gate).astype(o_ref.dtype)
        return 0

    lax.fori_loop(0, ts // c_len, chunk, 0)


def gla(proj, w2_pad, gb, og, ts):
    bsz, s, _ = proj.shape
    blk = jnp.arange(ts) // GLA_CHUNK
    tri = (jnp.tril(jnp.ones((ts, ts), F32)) * (blk[:, None] == blk[None, :])).astype(BF16)
    col = lambda c, w: pl.BlockSpec((bsz, ts, w), lambda i: (0, i, c // w))
    const = lambda shape: pl.BlockSpec(shape, lambda i: (0, 0))
    v_specs = [col(COL_GV + h * GLA_DV, GLA_DV) for h in range(GLA_HEADS)]
    r_specs = [col(COL_GR + h * GLA_DV, GLA_DV) for h in range(GLA_HEADS)]
    return pl.pallas_call(
        _gla_kernel,
        grid=(s // ts,),
        in_specs=[col(COL_GQ, GLA_KW), col(COL_GK, GLA_KW)] + v_specs + r_specs + [
            col(COL_SMALL, LANES), const((LANES, GLA_KW)), const((1, GLA_KW)), const((1, GLA_DV)), const((ts, ts))],
        out_specs=pl.BlockSpec((bsz, ts, GLA_VW), lambda i: (0, i, 0)),
        out_shape=jax.ShapeDtypeStruct((bsz, s, GLA_VW), BF16),
        scratch_shapes=[pltpu.VMEM((bsz * GLA_HEADS, GLA_DK, GLA_DV), F32), pltpu.VMEM((bsz, ts, GLA_KW), F32)],
        compiler_params=_params("arbitrary"),
    )(*([proj] * 11), w2_pad, gb, og, tri)


def _out_proj_kernel(h_ref, f_ref, c_ref, g_ref, w_ref, o_ref):
    fox = jnp.concatenate([f_ref[hd] for hd in range(FOX_HEADS)], axis=-1)
    acc = _dot(fox, w_ref[0:FOX_W, :])
    acc = acc + _dot(c_ref[...], w_ref[FOX_W:FOX_W + CONV_W, :])
    acc = acc + _dot(g_ref[...], w_ref[FOX_W + CONV_W:, :])
    o_ref[...] = h_ref[...] + acc


def out_proj(h, fox, conv, gla_o, w, tm):
    t, d = h.shape
    nt = fox.shape[2] // tm
    row = lambda w_: pl.BlockSpec((tm, w_), lambda i: (i, 0))
    heads = pl.BlockSpec((None, FOX_HEADS, tm, FOX_HEAD_DIM), lambda i: (i // nt, 0, i % nt, 0))
    return pl.pallas_call(
        _out_proj_kernel,
        grid=(t // tm,),
        in_specs=[row(d), heads, row(CONV_W), row(GLA_VW), pl.BlockSpec(w.shape, lambda i: (0, 0))],
        out_specs=row(d),
        out_shape=jax.ShapeDtypeStruct((t, d), F32),
        compiler_params=_params("parallel"),
    )(h, fox, conv, gla_o, w)


def _mem_kv_kernel(mem_ref, mg_ref, wk_ref, wv_ref, kg_ref, k_ref, v_ref):
    mn = _rms(mem_ref[...], mg_ref[...]).astype(BF16)
    k = _dot(mn, wk_ref[...])
    v_ref[...] = _dot(mn, wv_ref[...]).astype(v_ref.dtype)
    for h in range(XA_HEADS):
        hs = slice(h * XA_HEAD_DIM, (h + 1) * XA_HEAD_DIM)
        k_ref[:, hs] = _rms(k[:, hs], kg_ref[...]).astype(k_ref.dtype)


def mem_kv(mem, mem_g, wk, wv, kg):
    bsz, m, d = mem.shape
    nl, _, xw = wk.shape
    out = jax.ShapeDtypeStruct((nl, bsz, m, xw), BF16)
    o_spec = pl.BlockSpec((None, None, m, xw), lambda l, bi: (l, bi, 0, 0))
    w_spec = pl.BlockSpec((None, d, xw), lambda l, bi: (l, 0, 0))
    return pl.pallas_call(
        _mem_kv_kernel,
        grid=(nl, bsz),
        in_specs=[pl.BlockSpec((None, m, d), lambda l, bi: (bi, 0, 0)),
                  pl.BlockSpec((1, d), lambda l, bi: (0, 0)), w_spec, w_spec,
                  pl.BlockSpec((None, 1, XA_HEAD_DIM), lambda l, bi: (l, 0, 0))],
        out_specs=[o_spec, o_spec],
        out_shape=[out, out],
        compiler_params=_params("parallel", "parallel"),
    )(mem, mem_g, wk, wv, kg)


def _xattn_kernel(h_ref, g_ref, wq_ref, k_ref, v_ref, qg_ref, wo_ref, o_ref):
    h = h_ref[...]
    q = _dot(_rms(h, g_ref[...]).astype(BF16), wq_ref[...])
    scale = XA_HEAD_DIM ** -0.5
    outs = []
    for hd in range(XA_HEADS):
        hs = slice(hd * XA_HEAD_DIM, (hd + 1) * XA_HEAD_DIM)
        qn = (_rms(q[:, hs], qg_ref[...]) * scale).astype(BF16)
        s = _dot_nt(qn, k_ref[:, hs])
        p = jnp.exp(s - jnp.max(s, axis=-1, keepdims=True))
        denom = jnp.sum(p, axis=-1, keepdims=True)
        outs.append((_dot(p.astype(BF16), v_ref[:, hs]) / denom).astype(BF16))
    o_ref[...] = h + _dot(jnp.concatenate(outs, axis=-1), wo_ref[...])


def cross_attention(h, g, wq, k, v, qg, wo, tm):
    bsz, s, d = h.shape
    m, xw = k.shape[-2:]
    row = pl.BlockSpec((None, tm, d), lambda bi, i: (bi, i, 0))
    const = lambda shape: pl.BlockSpec(shape, lambda bi, i: (0, 0))
    kv = pl.BlockSpec((None, m, xw), lambda bi, i: (bi, 0, 0))
    return pl.pallas_call(
        _xattn_kernel,
        grid=(bsz, s // tm),
        in_specs=[row, const((1, d)), const((d, xw)), kv, kv, const((1, XA_HEAD_DIM)), const((xw, d))],
        out_specs=row,
        out_shape=jax.ShapeDtypeStruct((bsz, s, d), F32),
        compiler_params=_params("parallel", "parallel"),
    )(h, g, wq, k, v, qg, wo)


FFN_OUT_CHUNK = 512


def _ffn_body(xb_ref, wg_ref, wu_ref, wd_ref, acc_ref):
    xb = xb_ref[...]
    a = (_silu(_dot(xb, wg_ref[...])) * _dot(xb, wu_ref[...])).astype(BF16)
    for c in range(0, acc_ref.shape[-1], FFN_OUT_CHUNK):
        acc_ref[:, c:c + FFN_OUT_CHUNK] += _dot(a, wd_ref[:, c:c + FFN_OUT_CHUNK])


def _dense_ffn_kernel(h_ref, g_ref, wg_ref, wu_ref, wd_ref, o_ref, xb_ref):
    @pl.when(pl.program_id(1) == 0)
    def _():
        xb_ref[...] = _rms(h_ref[...], g_ref[...]).astype(BF16)
        o_ref[...] = h_ref[...]

    _ffn_body(xb_ref, wg_ref, wu_ref, wd_ref, o_ref)


def dense_ffn(h, g, wg, wu, wd, tm, tf):
    t, d = h.shape
    f = wg.shape[1]
    row = pl.BlockSpec((tm, d), lambda i, j: (i, 0))
    return pl.pallas_call(
        _dense_ffn_kernel,
        grid=(t // tm, f // tf),
        in_specs=[row, pl.BlockSpec((1, d), lambda i, j: (0, 0)),
                  pl.BlockSpec((d, tf), lambda i, j: (0, j)), pl.BlockSpec((d, tf), lambda i, j: (0, j)),
                  pl.BlockSpec((tf, d), lambda i, j: (j, 0))],
        out_specs=pl.BlockSpec((tm, d), lambda i, j: (i, 0), pipeline_mode=pl.Buffered(1)),
        out_shape=jax.ShapeDtypeStruct((t, d), F32),
        scratch_shapes=[pltpu.VMEM((tm, d), BF16)],
        compiler_params=_params("parallel", "arbitrary"),
    )(h, g, wg, wu, wd)


def _expert_ffn_kernel(te_ref, nu_ref, x_ref, wg_ref, wu_ref, wd_ref, o_ref, xb_ref):
    del te_ref

    @pl.when(pl.program_id(1) == 0)
    def _():
        xb_ref[...] = x_ref[...].astype(BF16)
        o_ref[...] = jnp.zeros_like(o_ref)

    @pl.when(pl.program_id(0) < nu_ref[0])
    def _():
        _ffn_body(xb_ref, wg_ref, wu_ref, wd_ref, o_ref)


def expert_ffn(xs, tile_expert, n_used, wg, wu, wd, tm, tf):
    r, d = xs.shape
    f = wg.shape[-1]
    nj = f // tf
    ti = lambda i, nu: jnp.minimum(i, nu[0] - 1)
    tj = lambda i, j, nu: jnp.where(i < nu[0], j, nj - 1)
    row = pl.BlockSpec((tm, d), lambda i, j, te, nu: (ti(i, nu), 0))
    w_in = pl.BlockSpec((None, d, tf), lambda i, j, te, nu: (te[ti(i, nu)], 0, tj(i, j, nu)))
    w_dn = pl.BlockSpec((None, tf, d), lambda i, j, te, nu: (te[ti(i, nu)], tj(i, j, nu), 0))
    return pl.pallas_call(
        _expert_ffn_kernel,
        grid_spec=pltpu.PrefetchScalarGridSpec(
            num_scalar_prefetch=2,
            grid=(r // tm, nj),
            in_specs=[row, w_in, w_in, w_dn],
            out_specs=pl.BlockSpec((tm, d), lambda i, j, te, nu: (i, 0), pipeline_mode=pl.Buffered(1)),
            scratch_shapes=[pltpu.VMEM((tm, d), BF16)],
        ),
        out_shape=jax.ShapeDtypeStruct((r, d), F32),
        compiler_params=_params("arbitrary", "arbitrary"),
    )(tile_expert, n_used, xs, wg, wu, wd)


def _router_kernel(h_ref, g_ref, rw_ref, rb_ref, tri_ref, hn_ref, info_ref, cnt_ref, carry_ref):
    @pl.when(pl.program_id(0) == 0)
    def _():
        carry_ref[...] = jnp.zeros_like(carry_ref)

    hn = _rms(h_ref[...], g_ref[...])
    hn_ref[...] = hn
    tm = hn.shape[0]
    x3 = _split3(hn)
    w3 = _split3(rw_ref[...])
    logits = rb_ref[...]
    for a, bpart in ((1, 0), (0, 1), (0, 0)):
        logits = logits + _dot(x3[a], w3[bpart])
    lane = lax.broadcasted_iota(jnp.int32, (tm, LANES), 1).astype(F32)
    logits = jnp.where(lane < N_EXPERTS, logits, NEG_BIG)
    v_a = jnp.max(logits, axis=-1, keepdims=True)
    e_a = jnp.min(jnp.where(logits == v_a, lane, float(LANES)), axis=-1, keepdims=True)
    rest = jnp.where(lane == e_a, NEG_BIG, logits)
    v_b = jnp.max(rest, axis=-1, keepdims=True)
    e_b = jnp.min(jnp.where(rest == v_b, lane, float(LANES)), axis=-1, keepdims=True)
    t = jnp.exp(v_b - v_a)
    w_a = 1.0 / (1.0 + t)
    w_b = t * w_a
    chosen = (lane == e_a) | (lane == e_b)
    onehot = jnp.where(chosen, 1.0, 0.0)
    before = _dot(tri_ref[...], onehot.astype(BF16)) + carry_ref[...]
    carry_ref[...] += jnp.sum(onehot, axis=0, keepdims=True)
    cnt_ref[...] = carry_ref[...]
    rank_a = jnp.sum(jnp.where(lane == e_a, before, 0.0), axis=-1, keepdims=True)
    rank_b = jnp.sum(jnp.where(lane == e_b, before, 0.0), axis=-1, keepdims=True)
    info_ref[...] = jnp.where(lane == 0, e_a, jnp.where(lane == 1, e_b, jnp.where(
        lane == 2, w_a, jnp.where(lane == 3, w_b, jnp.where(lane == 4, rank_a, jnp.where(lane == 5, rank_b, 0.0))))))


def router(h, g, rw_pad, rb_pad, tm):
    t, d = h.shape
    tri = jnp.tril(jnp.ones((tm, tm), F32), -1).astype(BF16)
    row = lambda w: pl.BlockSpec((tm, w), lambda i: (i, 0))
    const = lambda shape: pl.BlockSpec(shape, lambda i: (0, 0))
    return pl.pallas_call(
        _router_kernel,
        grid=(t // tm,),
        in_specs=[row(d), const((1, d)), const((d, LANES)), const((1, LANES)), const((tm, tm))],
        out_specs=[row(d), row(LANES), const((1, LANES))],
        out_shape=[jax.ShapeDtypeStruct((t, d), F32), jax.ShapeDtypeStruct((t, LANES), F32),
                   jax.ShapeDtypeStruct((1, LANES), F32)],
        scratch_shapes=[pltpu.VMEM((1, LANES), F32)],
        compiler_params=_params("arbitrary"),
    )(h, g, rw_pad, rb_pad, tri)


def _row_copy(src_ref, src_row, dst_ref, dst_row, sem):
    return pltpu.make_async_copy(src_ref.at[pl.ds(src_row, 1), :], dst_ref.at[pl.ds(dst_row, 1), :], sem)


def _dispatch_kernel(da_ref, db_ref, gs_ref, gl_ref, x_ref, xs_ref, zero_ref, sem):
    tm = x_ref.shape[0]
    base = pl.program_id(0) * tm

    @pl.when(pl.program_id(0) == 0)
    def _():
        zero_ref[...] = jnp.zeros_like(zero_ref)
        for g in range(gs_ref.shape[0]):
            def fill(r, _, g=g):
                _row_copy(zero_ref, 0, xs_ref, gs_ref[g] + r, sem).start()
                return 0

            def filled(r, _):
                _row_copy(zero_ref, 0, xs_ref, 0, sem).wait()
                return 0

            lax.fori_loop(0, gl_ref[g], fill, 0)
            lax.fori_loop(0, gl_ref[g], filled, 0)

    def issue(r, _):
        _row_copy(x_ref, r, xs_ref, da_ref[base + r], sem).start()
        _row_copy(x_ref, r, xs_ref, db_ref[base + r], sem).start()
        return 0

    def drain(r, _):
        _row_copy(x_ref, 0, xs_ref, 0, sem).wait()
        _row_copy(x_ref, 0, xs_ref, 0, sem).wait()
        return 0

    lax.fori_loop(0, tm, issue, 0, unroll=8)
    lax.fori_loop(0, tm, drain, 0, unroll=8)


def dispatch(x, dest_a, dest_b, gap_start, gap_len, n_rows, tm):
    t, d = x.shape
    return pl.pallas_call(
        _dispatch_kernel,
        grid_spec=pltpu.PrefetchScalarGridSpec(
            num_scalar_prefetch=4,
            grid=(t // tm,),
            in_specs=[pl.BlockSpec((tm, d), lambda i, *_: (i, 0))],
            out_specs=pl.BlockSpec(memory_space=pl.ANY),
            scratch_shapes=[pltpu.VMEM((SUBLANES, d), x.dtype), pltpu.SemaphoreType.DMA(())],
        ),
        out_shape=jax.ShapeDtypeStruct((n_rows, d), x.dtype),
        compiler_params=_params("arbitrary"),
    )(dest_a, dest_b, gap_start, gap_len, x)


def _combine_kernel(da_ref, db_ref, h_ref, info_ref, ys_ref, o_ref, ya_ref, yb_ref, sem):
    tm = h_ref.shape[0]
    base = pl.program_id(0) * tm

    def issue(r, _):
        _row_copy(ys_ref, da_ref[base + r], ya_ref, r, sem).start()
        _row_copy(ys_ref, db_ref[base + r], yb_ref, r, sem).start()
        return 0

    def drain(r, _):
        _row_copy(ys_ref, 0, ya_ref, 0, sem).wait()
        _row_copy(ys_ref, 0, yb_ref, 0, sem).wait()
        return 0

    lax.fori_loop(0, tm, issue, 0, unroll=8)
    lax.fori_loop(0, tm, drain, 0, unroll=8)
    info = info_ref[...]
    o_ref[...] = h_ref[...] + info[:, 2:3] * ya_ref[...] + info[:, 3:4] * yb_ref[...]


def combine(h, info, ys, dest_a, dest_b, tm):
    t, d = h.shape
    row = lambda w: pl.BlockSpec((tm, w), lambda i, da, db: (i, 0))
    return pl.pallas_call(
        _combine_kernel,
        grid_spec=pltpu.PrefetchScalarGridSpec(
            num_scalar_prefetch=2,
            grid=(t // tm,),
            in_specs=[row(d), row(LANES), pl.BlockSpec(memory_space=pl.ANY)],
            out_specs=row(d),
            scratch_shapes=[pltpu.VMEM((tm, d), F32), pltpu.VMEM((tm, d), F32), pltpu.SemaphoreType.DMA(())],
        ),
        out_shape=jax.ShapeDtypeStruct((t, d), F32),
        compiler_params=_params("arbitrary"),
    )(dest_a, dest_b, h, info, ys)


def moe_ffn(h, g, rw, rb, wg, wu, wd, tm_route, tm_rows, tf, tm_move):
    t, d = h.shape
    rw_pad = jnp.zeros((d, LANES), F32).at[:, :N_EXPERTS].set(rw)
    rb_pad = jnp.zeros((1, LANES), F32).at[0, :N_EXPERTS].set(rb)
    hn, info, counts = router(h, g, rw_pad, rb_pad, tm_route)
    counts = counts[0, :N_EXPERTS].astype(jnp.int32)
    padded = (counts + tm_rows - 1) // tm_rows * tm_rows
    ends = jnp.cumsum(padded)
    starts = ends - padded
    e_a, e_b = info[:, 0].astype(jnp.int32), info[:, 1].astype(jnp.int32)
    dest_a = starts[e_a] + info[:, 4].astype(jnp.int32)
    dest_b = starts[e_b] + info[:, 5].astype(jnp.int32)
    n_rows = 2 * t + N_EXPERTS * tm_rows
    n_tiles = n_rows // tm_rows
    n_used = (ends[-1] // tm_rows).astype(jnp.int32).reshape(1)
    tile_start = jnp.arange(n_tiles, dtype=jnp.int32) * tm_rows
    tile_expert = jnp.minimum(jnp.sum(tile_start[:, None] >= ends[None, :], axis=-1), N_EXPERTS - 1).astype(jnp.int32)
    gap_start = jnp.concatenate([starts + counts, ends[-1:]]).astype(jnp.int32)
    gap_len = jnp.concatenate([padded - counts, n_rows - ends[-1:]]).astype(jnp.int32)
    xs = dispatch(hn, dest_a, dest_b, gap_start, gap_len, n_rows, tm_move)
    ys = expert_ffn(xs, tile_expert, n_used, wg, wu, wd, tm_rows, tf)
    return combine(h, info, ys, dest_a, dest_b, tm_move)


def _pack_w_in(w):
    d = w.shape[0]
    pts = [0, 512, 1024, 1536, 1544, 2056, 2568, 3080, 3592, 4616, 4632, 5656]
    fq, fk, fv, ff, ca, cg, gq, gk, gv, glr, gr = [w[:, a:b] for a, b in zip(pts[:-1], pts[1:])]
    pad = jnp.zeros((d, LANES - FOX_HEADS - GLA_GATE_RANK), w.dtype)
    return jnp.concatenate([fq, fk, fv, ca, cg, gq, gk, gv, gr, ff, glr, pad], axis=1).astype(BF16)


def _tile(n, pref):
    return min(n, pref)


def kernel(x, mem, mem_norm, mix_norm, w_in, fox_forget_bias, fox_q_norm, fox_k_norm, conv_w, conv_b,
           conv_ln_g, conv_ln_b, gla_gate_w2, gla_gate_b, gla_out_norm, w_out, xa_norm, xa_wq, xa_wk, xa_wv,
           xa_q_norm, xa_k_norm, xa_wo, ffn_norm, dense_wg, dense_wu, dense_wd, router_w, router_b,
           moe_wg, moe_wu, moe_wd):
    bsz, s, d = x.shape
    t = bsz * s
    depth = mix_norm.shape[0]
    ts = _tile(s, 512)
    row = lambda v: v.reshape(1, -1).astype(F32)

    k_mem, v_mem = mem_kv(mem, row(mem_norm), xa_wk.astype(BF16), xa_wv.astype(BF16),
                          xa_k_norm.reshape(depth, 1, XA_HEAD_DIM))
    h = x.reshape(t, d)
    for l in range(depth):
        proj = norm_matmul(h, row(mix_norm[l]), _pack_w_in(w_in[l]), _tile(t, 1024), PACK_W // 3)
        proj = proj.reshape(bsz, s, PACK_W)
        bf_pad = jnp.zeros((1, LANES), F32).at[0, :FOX_HEADS].set(fox_forget_bias[l])
        qg, kg = row(fox_q_norm[l]), row(fox_k_norm[l])
        fox = fox_attention(proj, bf_pad, qg, kg, ts)
        conv_w_pad = jnp.zeros((CONV_HALO, CONV_W), F32).at[:CONV_TAPS].set(conv_w[l])
        conv = conformer_conv(proj, conv_w_pad, row(conv_b[l]), row(conv_ln_g[l]), row(conv_ln_b[l]), ts)
        w2_pad = jnp.zeros((LANES, GLA_KW), F32).at[FOX_HEADS:FOX_HEADS + GLA_GATE_RANK].set(gla_gate_w2[l])
        gla_o = gla(proj, w2_pad.astype(BF16), row(gla_gate_b[l]), row(gla_out_norm[l]), ts)
        h = out_proj(h, fox, conv.reshape(t, CONV_W), gla_o.reshape(t, GLA_VW), w_out[l].astype(BF16),
                     _tile(t, 512))
        h = cross_attention(h.reshape(bsz, s, d), row(xa_norm[l]), xa_wq[l].astype(BF16), k_mem[l], v_mem[l],
                            row(xa_q_norm[l]), xa_wo[l].astype(BF16), ts).reshape(t, d)
        j = l // 2
        if l % 2 == 0:
            h = dense_ffn(h, row(ffn_norm[l]), dense_wg[j].astype(BF16), dense_wu[j].astype(BF16),
                          dense_wd[j].astype(BF16), _tile(t, 1024), 512)
        else:
            h = moe_ffn(h, row(ffn_norm[l]), router_w[j], router_b[j], moe_wg[j].astype(BF16),
                        moe_wu[j].astype(BF16), moe_wd[j].astype(BF16), _tile(t, 512), _tile(t, 1024), 512,
                        _tile(t, 256))
    return h.reshape(bsz, s, d)
```

```python
import functools

import jax
import jax.numpy as jnp
from jax import lax
from jax.experimental import pallas as pl
from jax.experimental.pallas import tpu as pltpu

F32 = jnp.float32
BF16 = jnp.bfloat16
EPS = 1e-6

FOX_HEADS = 8
FOX_HEAD_DIM = 64
CONV_TAPS = 31
CONV_HALO = 32
GLA_HEADS = 4
GLA_DK = 128
GLA_DV = 256
GLA_GATE_RANK = 16
GLA_GATE_TAU = 16.0
GLA_CHUNK = 64
XA_HEADS = 4
XA_HEAD_DIM = 128
N_EXPERTS = 8
LANES = 128
SUBLANES = 8

FOX_W = FOX_HEADS * FOX_HEAD_DIM
CONV_W = 512
GLA_KW = GLA_HEADS * GLA_DK
GLA_VW = GLA_HEADS * GLA_DV
COL_FQ, COL_FK, COL_FV = 0, 512, 1024
COL_CA, COL_CG = 1536, 2048
COL_GQ, COL_GK, COL_GV, COL_GR = 2560, 3072, 3584, 4608
COL_SMALL = 5632
PACK_W = COL_SMALL + LANES

FOX_SKIP_MARGIN = 110.0
NEG_BIG = -1e30
FOX_CHAINS = 4
FOX_RANGE_LIMIT = 100.0

VMEM_LIMIT = 56 * 1024 * 1024


def _params(*sem):
    return pltpu.CompilerParams(dimension_semantics=sem, vmem_limit_bytes=VMEM_LIMIT)


def _dot(a, b):
    return jnp.dot(a, b, preferred_element_type=F32)


def _dot_nt(a, b):
    return lax.dot_general(a, b, (((1,), (1,)), ((), ())), preferred_element_type=F32)


def _split3(x):
    hi = x.astype(BF16)
    r = x - hi.astype(F32)
    mid = r.astype(BF16)
    lo = (r - mid.astype(F32)).astype(BF16)
    return hi, mid, lo


def _log_sigmoid(z):
    return jnp.minimum(z, 0.0) - jnp.log1p(jnp.exp(-jnp.abs(z)))


def _silu(z):
    return z * (1.0 / (1.0 + jnp.exp(-z)))


def _rms(x, g):
    ms = jnp.mean(x * x, axis=-1, keepdims=True)
    return x * lax.rsqrt(ms + EPS) * g


def _norm_mm_kernel(h_ref, g_ref, w_ref, o_ref, hn_ref):
    @pl.when(pl.program_id(1) == 0)
    def _():
        hn_ref[...] = _rms(h_ref[...], g_ref[...]).astype(BF16)

    o_ref[...] = _dot(hn_ref[...], w_ref[...]).astype(o_ref.dtype)


def norm_matmul(h, g, w, tm, tn):
    t, d = h.shape
    n = w.shape[1]
    return pl.pallas_call(
        _norm_mm_kernel,
        grid=(t // tm, n // tn),
        in_specs=[pl.BlockSpec((tm, d), lambda i, j: (i, 0)),
                  pl.BlockSpec((1, d), lambda i, j: (0, 0)),
                  pl.BlockSpec((d, tn), lambda i, j: (0, j))],
        out_specs=pl.BlockSpec((tm, tn), lambda i, j: (i, j)),
        out_shape=jax.ShapeDtypeStruct((t, n), BF16),
        scratch_shapes=[pltpu.VMEM((tm, d), BF16)],
        compiler_params=_params("parallel", "arbitrary"),
    )(h, g, w)


LOG2E = 1.4426950408889634


def _fox_prep_kernel(q_ref, k_ref, v_ref, sm_ref, bf_ref, qg_ref, kg_ref, tri_ref,
                     qa_ref, ka_ref, va_ref, cf_ref, cl_ref, carry_ref):
    ts = q_ref.shape[0]

    @pl.when(pl.program_id(1) == 0)
    def _():
        carry_ref[...] = jnp.zeros_like(carry_ref)

    lf = _log_sigmoid(sm_ref[...].astype(F32) + bf_ref[...])
    hi, mid, lo = _split3(lf)
    tri = tri_ref[...]
    cum = _dot(tri, hi) + _dot(tri, mid) + _dot(tri, lo) + carry_ref[...]
    carry_ref[...] = cum[ts - 1:ts, :]
    cf_ref[...] = cum[0:1, :]
    cl_ref[...] = cum[ts - 1:ts, :]

    lane = lax.broadcasted_iota(jnp.int32, (ts, FOX_HEAD_DIM), 1)
    scale = FOX_HEAD_DIM ** -0.5 * LOG2E
    cum2 = cum * LOG2E
    for h in range(FOX_HEADS):
        sl = slice(h * FOX_HEAD_DIM, (h + 1) * FOX_HEAD_DIM)
        c = cum2[:, h:h + 1]
        c_hi = c.astype(BF16).astype(F32)
        r = c - c_hi
        c_mid = r.astype(BF16).astype(F32)
        c_lo = r - c_mid
        qn = _rms(q_ref[:, sl].astype(F32), qg_ref[...]) * scale
        kn = _rms(k_ref[:, sl].astype(F32), kg_ref[...])
        q_extra = jnp.where(lane == 0, c_hi, jnp.where(lane == 1, c_mid, jnp.where(
            lane == 2, c_lo, jnp.where(lane < 6, 1.0, 0.0))))
        k_extra = jnp.where(lane < 3, 1.0, jnp.where(lane == 3, -c_hi, jnp.where(
            lane == 4, -c_mid, jnp.where(lane == 5, -c_lo, 0.0))))
        v_extra = jnp.where(lane == 0, 1.0, 0.0)
        qa_ref[h] = jnp.concatenate([qn, q_extra], axis=-1).astype(BF16)
        ka_ref[h] = jnp.concatenate([kn, k_extra], axis=-1).astype(BF16)
        va_ref[h] = jnp.concatenate([v_ref[:, sl].astype(F32), v_extra], axis=-1).astype(BF16)


def fox_prep(proj, bf_pad, qg, kg, ts):
    b, s, _ = proj.shape
    nt = s // ts
    tri = jnp.tril(jnp.ones((ts, ts), F32)).astype(BF16)
    aug = jax.ShapeDtypeStruct((b, FOX_HEADS, s, LANES), BF16)
    edge = jax.ShapeDtypeStruct((b, nt, 1, LANES), F32)
    col = lambda c, w: pl.BlockSpec((None, ts, w), lambda bi, i: (bi, i, c // w))
    aug_spec = pl.BlockSpec((None, FOX_HEADS, ts, LANES), lambda bi, i: (bi, 0, i, 0))
    edge_spec = pl.BlockSpec((None, None, 1, LANES), lambda bi, i: (bi, i, 0, 0))
    const = lambda shape: pl.BlockSpec(shape, lambda bi, i: (0,) * len(shape))
    return pl.pallas_call(
        _fox_prep_kernel,
        grid=(b, nt),
        in_specs=[col(COL_FQ, FOX_W), col(COL_FK, FOX_W), col(COL_FV, FOX_W), col(COL_SMALL, LANES),
                  const((1, LANES)), const((1, FOX_HEAD_DIM)), const((1, FOX_HEAD_DIM)), const((ts, ts))],
        out_specs=[aug_spec, aug_spec, aug_spec, edge_spec, edge_spec],
        out_shape=[aug, aug, aug, edge, edge],
        scratch_shapes=[pltpu.VMEM((1, LANES), F32)],
        compiler_params=_params("parallel", "arbitrary"),
    )(proj, proj, proj, proj, bf_pad, qg, kg, tri)


def _fox_flash_kernel(jlo_ref, q_ref, k_ref, v_ref, o_ref, *, bounded):
    tq = q_ref.shape[0] // 2
    i = pl.program_id(2)
    flat = (pl.program_id(0) * pl.num_programs(1) + pl.program_id(1)) * pl.num_programs(2) + i
    rc = 2 * tq // FOX_CHAINS
    qs = [q_ref[c * rc:(c + 1) * rc, :] for c in range(FOX_CHAINS)]
    qpos = lax.broadcasted_iota(jnp.int32, (rc, tq), 0)
    kpos = lax.broadcasted_iota(jnp.int32, (rc, tq), 1)

    def sweep(j, carry, chains, key_offset=None):
        rows = pl.ds(pl.multiple_of(j * tq, tq), tq)
        k = k_ref[rows, :]
        v = v_ref[rows, :]
        logits = [_dot_nt(qs[c], k) for c in chains]
        carry = list(carry)
        for c, s in zip(chains, logits):
            if key_offset is not None and key_offset + tq > c * rc:
                s = jnp.where(kpos + key_offset <= qpos + c * rc, s, NEG_BIG)
            if bounded:
                carry[c] = carry[c] + _dot(jnp.exp2(s).astype(BF16), v)
            else:
                m, acc = carry[c]
                m_new = jnp.maximum(m, jnp.max(s, axis=-1, keepdims=True))
                p = jnp.exp2(s - m_new).astype(BF16)
                carry[c] = (m_new, jnp.exp2(m - m_new) * acc + _dot(p, v))
        return tuple(carry)

    every = tuple(range(FOX_CHAINS))
    acc0 = jnp.zeros((rc, LANES), F32)
    init = acc0 if bounded else (jnp.full((rc, 1), NEG_BIG, F32), acc0)
    carry = lax.fori_loop(jlo_ref[flat], 2 * i, lambda j, c: sweep(j, c, every), (init,) * FOX_CHAINS)
    carry = sweep(2 * i, carry, every, 0)
    carry = sweep(2 * i + 1, carry, every[FOX_CHAINS // 2:], tq)
    for c, state in enumerate(carry):
        acc = state if bounded else state[1]
        out = acc[:, :FOX_HEAD_DIM] / acc[:, FOX_HEAD_DIM:FOX_HEAD_DIM + 1]
        o_ref[c * rc:(c + 1) * rc, :] = out.astype(o_ref.dtype)


def fox_flash(q_aug, k_aug, v_aug, jlo_pair, tq, bounded):
    b, nh, s, _ = q_aug.shape
    full = pl.BlockSpec((None, None, s, LANES), lambda bi, h, i, jl: (bi, h, 0, 0))
    return pl.pallas_call(
        functools.partial(_fox_flash_kernel, bounded=bounded),
        grid_spec=pltpu.PrefetchScalarGridSpec(
            num_scalar_prefetch=1,
            grid=(b, nh, s // (2 * tq)),
            in_specs=[pl.BlockSpec((None, None, 2 * tq, LANES), lambda bi, h, i, jl: (bi, h, i, 0)), full, full],
            out_specs=pl.BlockSpec((None, None, 2 * tq, FOX_HEAD_DIM), lambda bi, h, i, jl: (bi, h, i, 0)),
        ),
        out_shape=jax.ShapeDtypeStruct((b, nh, s, FOX_HEAD_DIM), BF16),
        compiler_params=_params("parallel", "parallel", "arbitrary"),
    )(jlo_pair, q_aug, k_aug, v_aug)


def fox_first_block(c_first, c_last, qk_bound):
    cf = c_first[:, :, 0, :FOX_HEADS].transpose(0, 2, 1)
    cl = c_last[:, :, 0, :FOX_HEADS].transpose(0, 2, 1)
    nq = cf.shape[-1]
    gap = cf[..., :, None] - cl[..., None, :]
    j_idx = jnp.arange(nq)
    skippable = (qk_bound + gap < -FOX_SKIP_MARGIN) & (j_idx[None, :] < j_idx[:, None])
    first = jnp.sum(skippable, axis=-1).astype(jnp.int32)
    return jnp.minimum(first[..., 0::2], first[..., 1::2]).reshape(-1)


def fox_attention(proj, bf_pad, qg, kg, ts):
    qk_bound = 2.0 * FOX_HEAD_DIM ** 0.5 * jnp.max(jnp.abs(qg)) * jnp.max(jnp.abs(kg)) * (1.0 + 1e-3)
    q_aug, k_aug, v_aug, c_first, c_last = fox_prep(proj, bf_pad, qg, kg, ts)
    jlo = fox_first_block(c_first, c_last, qk_bound)
    sweep = lambda bounded: functools.partial(fox_flash, tq=ts, bounded=bounded)
    return lax.cond(qk_bound * LOG2E <= FOX_RANGE_LIMIT, sweep(True), sweep(False), q_aug, k_aug, v_aug, jlo)


def _conv_kernel(a_ref, g_ref, ap_ref, gp_ref, w_ref, b_ref, lg_ref, lb_ref, o_ref, u_ref):
    ts = a_ref.shape[0]

    def glu(a, g):
        g = g.astype(F32)
        return a.astype(F32) * (1.0 / (1.0 + jnp.exp(-g)))

    prev = glu(ap_ref[...], gp_ref[...])
    u_ref[0, 0:CONV_HALO, :] = jnp.where(pl.program_id(1) > 0, prev, 0.0)
    u_ref[0, CONV_HALO:CONV_HALO + ts, :] = glu(a_ref[...], g_ref[...])
    u_ref[0, CONV_HALO + ts:, :] = jnp.zeros((SUBLANES, CONV_W), F32)
    base = u_ref[0]
    for r in range(1, SUBLANES):
        u_ref[r] = pltpu.roll(base, base.shape[0] - r, axis=0)
    first = CONV_HALO - (CONV_TAPS - 1)
    y = jnp.zeros((ts, CONV_W), F32) + b_ref[...]
    for j in range(CONV_TAPS):
        r, start = (first + j) % SUBLANES, (first + j) // SUBLANES * SUBLANES
        y = y + w_ref[j:j + 1, :] * u_ref[r, start:start + ts, :]
    mu = jnp.mean(y, axis=-1, keepdims=True)
    yc = y - mu
    var = jnp.mean(yc * yc, axis=-1, keepdims=True)
    y = yc * lax.rsqrt(var + EPS) * lg_ref[...] + lb_ref[...]
    o_ref[...] = _silu(y).astype(o_ref.dtype)


def conformer_conv(proj, w_pad, b, ln_g, ln_b, ts):
    bsz, s, _ = proj.shape
    ratio = ts // CONV_HALO
    main = lambda c: pl.BlockSpec((None, ts, CONV_W), lambda bi, i: (bi, i, c // CONV_W))
    halo = lambda c: pl.BlockSpec((None, CONV_HALO, CONV_W),
                                  lambda bi, i: (bi, jnp.maximum(i * ratio - 1, 0), c // CONV_W))
    const = lambda shape: pl.BlockSpec(shape, lambda bi, i: (0, 0))
    return pl.pallas_call(
        _conv_kernel,
        grid=(bsz, s // ts),
        in_specs=[main(COL_CA), main(COL_CG), halo(COL_CA), halo(COL_CG),
                  const((CONV_HALO, CONV_W)), const((1, CONV_W)), const((1, CONV_W)), const((1, CONV_W))],
        out_specs=pl.BlockSpec((None, ts, CONV_W), lambda bi, i: (bi, i, 0)),
        out_shape=jax.ShapeDtypeStruct((bsz, s, CONV_W), BF16),
        scratch_shapes=[pltpu.VMEM((SUBLANES, CONV_HALO + ts + SUBLANES, CONV_W), F32)],
        compiler_params=_params("parallel", "parallel"),
    )(proj, proj, proj, proj, w_pad, b, ln_g, ln_b)


def _gla_kernel(q_ref, k_ref, v0_ref, v1_ref, v2_ref, v3_ref, r0_ref, r1_ref, r2_ref, r3_ref,
                sm_ref, w2_ref, gb_ref, og_ref, tri_ref, o_ref, state_ref, bcum_ref):
    bsz, ts = q_ref.shape[0], q_ref.shape[1]
    c_len = GLA_CHUNK
    v_refs = (v0_ref, v1_ref, v2_ref, v3_ref)
    r_refs = (r0_ref, r1_ref, r2_ref, r3_ref)

    @pl.when(pl.program_id(0) == 0)
    def _():
        state_ref[...] = jnp.zeros_like(state_ref)

    tri = tri_ref[...]
    for b in range(bsz):
        log_a = _log_sigmoid(_dot(sm_ref[b], w2_ref[...]) + gb_ref[...]) * (1.0 / GLA_GATE_TAU)
        hi, mid, lo = _split3(log_a)
        bcum_ref[b] = _dot(tri, hi) + _dot(tri, mid) + _dot(tri, lo)

    rr = lax.broadcasted_iota(jnp.int32, (c_len, c_len), 0)
    cc = lax.broadcasted_iota(jnp.int32, (c_len, c_len), 1)
    causal = cc <= rr
    scale = GLA_DK ** -0.5
    chains = [(b, h) for b in range(bsz) for h in range(GLA_HEADS)]

    def chunk(ci, _):
        rows = pl.ds(pl.multiple_of(ci * c_len, c_len), c_len)
        prep = []
        for b, h in chains:
            hs = slice(h * GLA_DK, (h + 1) * GLA_DK)
            bc = bcum_ref[b, rows, hs]
            b_mid = bc[c_len // 2:c_len // 2 + 1, :]
            b_last = bc[c_len - 1:c_len, :]
            q = q_ref[b, rows, hs].astype(F32) * scale
            k = k_ref[b, rows, hs].astype(F32)
            q_rel = (q * jnp.exp(bc - b_mid)).astype(BF16)
            k_rel = (k * jnp.exp(b_mid - bc)).astype(BF16)
            q_in = (q * jnp.exp(bc)).astype(BF16)
            k_out_t = jnp.transpose(k * jnp.exp(b_last - bc)).astype(BF16)
            decay = jnp.transpose(jnp.broadcast_to(jnp.exp(b_last), (8, GLA_DK)))[:, 0:1]
            prep.append((q_rel, k_rel, q_in, k_out_t, decay))
        scores = [jnp.where(causal, _dot_nt(p[0], p[1]), 0.0).astype(BF16) for p in prep]
        states = [state_ref[b * GLA_HEADS + h] for b, h in chains]
        inter = [_dot(p[2], st.astype(BF16)) for p, st in zip(prep, states)]
        vals = [v_refs[h][b, rows, :] for b, h in chains]
        outs = [_dot(sc, v) + it for sc, v, it in zip(scores, vals, inter)]
        for (b, h), p, st, v in zip(chains, prep, states, vals):
            state_ref[b * GLA_HEADS + h] = p[4] * st + _dot(p[3], v)
        for (b, h), o in zip(chains, outs):
            gate = _silu(r_refs[h][b, rows, :].astype(F32))
            o_ref[b, rows, h * GLA_DV:(h + 1) * GLA_DV] = (_rms(o, og_ref[...]) * gate).astype(o_ref.dtype)
        return 0

    lax.fori_loop(0, ts // c_len, chunk, 0)


def gla(proj, w2_pad, gb, og, ts):
    bsz, s, _ = proj.shape
    blk = jnp.arange(ts) // GLA_CHUNK
    tri = (jnp.tril(jnp.ones((ts, ts), F32)) * (blk[:, None] == blk[None, :])).astype(BF16)
    col = lambda c, w: pl.BlockSpec((bsz, ts, w), lambda i: (0, i, c // w))
    const = lambda shape: pl.BlockSpec(shape, lambda i: (0, 0))
    v_specs = [col(COL_GV + h * GLA_DV, GLA_DV) for h in range(GLA_HEADS)]
    r_specs = [col(COL_GR + h * GLA_DV, GLA_DV) for h in range(GLA_HEADS)]
    return pl.pallas_call(
        _gla_kernel,
        grid=(s // ts,),
        in_specs=[col(COL_GQ, GLA_KW), col(COL_GK, GLA_KW)] + v_specs + r_specs + [
            col(COL_SMALL, LANES), const((LANES, GLA_KW)), const((1, GLA_KW)), const((1, GLA_DV)), const((ts, ts))],
        out_specs=pl.BlockSpec((bsz, ts, GLA_VW), lambda i: (0, i, 0)),
        out_shape=jax.ShapeDtypeStruct((bsz, s, GLA_VW), BF16),
        scratch_shapes=[pltpu.VMEM((bsz * GLA_HEADS, GLA_DK, GLA_DV), F32), pltpu.VMEM((bsz, ts, GLA_KW), F32)],
        compiler_params=_params("arbitrary"),
    )(*([proj] * 11), w2_pad, gb, og, tri)


def _out_proj_kernel(h_ref, f_ref, c_ref, g_ref, w_ref, o_ref):
    fox = jnp.concatenate([f_ref[hd] for hd in range(FOX_HEADS)], axis=-1)
    acc = _dot(fox, w_ref[0:FOX_W, :])
    acc = acc + _dot(c_ref[...], w_ref[FOX_W:FOX_W + CONV_W, :])
    acc = acc + _dot(g_ref[...], w_ref[FOX_W + CONV_W:, :])
    o_ref[...] = h_ref[...] + acc


def out_proj(h, fox, conv, gla_o, w, tm):
    t, d = h.shape
    nt = fox.shape[2] // tm
    row = lambda w_: pl.BlockSpec((tm, w_), lambda i: (i, 0))
    heads = pl.BlockSpec((None, FOX_HEADS, tm, FOX_HEAD_DIM), lambda i: (i // nt, 0, i % nt, 0))
    return pl.pallas_call(
        _out_proj_kernel,
        grid=(t // tm,),
        in_specs=[row(d), heads, row(CONV_W), row(GLA_VW), pl.BlockSpec(w.shape, lambda i: (0, 0))],
        out_specs=row(d),
        out_shape=jax.ShapeDtypeStruct((t, d), F32),
        compiler_params=_params("parallel"),
    )(h, fox, conv, gla_o, w)


def _mem_kv_kernel(mem_ref, mg_ref, wk_ref, wv_ref, kg_ref, k_ref, v_ref):
    mn = _rms(mem_ref[...], mg_ref[...]).astype(BF16)
    k = _dot(mn, wk_ref[...])
    v_ref[...] = _dot(mn, wv_ref[...]).astype(v_ref.dtype)
    for h in range(XA_HEADS):
        hs = slice(h * XA_HEAD_DIM, (h + 1) * XA_HEAD_DIM)
        k_ref[:, hs] = _rms(k[:, hs], kg_ref[...]).astype(k_ref.dtype)


def mem_kv(mem, mem_g, wk, wv, kg):
    bsz, m, d = mem.shape
    nl, _, xw = wk.shape
    out = jax.ShapeDtypeStruct((nl, bsz, m, xw), BF16)
    o_spec = pl.BlockSpec((None, None, m, xw), lambda l, bi: (l, bi, 0, 0))
    w_spec = pl.BlockSpec((None, d, xw), lambda l, bi: (l, 0, 0))
    return pl.pallas_call(
        _mem_kv_kernel,
        grid=(nl, bsz),
        in_specs=[pl.BlockSpec((None, m, d), lambda l, bi: (bi, 0, 0)),
                  pl.BlockSpec((1, d), lambda l, bi: (0, 0)), w_spec, w_spec,
                  pl.BlockSpec((None, 1, XA_HEAD_DIM), lambda l, bi: (l, 0, 0))],
        out_specs=[o_spec, o_spec],
        out_shape=[out, out],
        compiler_params=_params("parallel", "parallel"),
    )(mem, mem_g, wk, wv, kg)


def _xattn_kernel(h_ref, g_ref, wq_ref, k_ref, v_ref, qg_ref, wo_ref, o_ref):
    h = h_ref[...]
    q = _dot(_rms(h, g_ref[...]).astype(BF16), wq_ref[...])
    scale = XA_HEAD_DIM ** -0.5
    outs = []
    for hd in range(XA_HEADS):
        hs = slice(hd * XA_HEAD_DIM, (hd + 1) * XA_HEAD_DIM)
        qn = (_rms(q[:, hs], qg_ref[...]) * scale).astype(BF16)
        s = _dot_nt(qn, k_ref[:, hs])
        p = jnp.exp(s - jnp.max(s, axis=-1, keepdims=True))
        denom = jnp.sum(p, axis=-1, keepdims=True)
        outs.append((_dot(p.astype(BF16), v_ref[:, hs]) / denom).astype(BF16))
    o_ref[...] = h + _dot(jnp.concatenate(outs, axis=-1), wo_ref[...])


def cross_attention(h, g, wq, k, v, qg, wo, tm):
    bsz, s, d = h.shape
    m, xw = k.shape[-2:]
    row = pl.BlockSpec((None, tm, d), lambda bi, i: (bi, i, 0))
    const = lambda shape: pl.BlockSpec(shape, lambda bi, i: (0, 0))
    kv = pl.BlockSpec((None, m, xw), lambda bi, i: (bi, 0, 0))
    return pl.pallas_call(
        _xattn_kernel,
        grid=(bsz, s // tm),
        in_specs=[row, const((1, d)), const((d, xw)), kv, kv, const((1, XA_HEAD_DIM)), const((xw, d))],
        out_specs=row,
        out_shape=jax.ShapeDtypeStruct((bsz, s, d), F32),
        compiler_params=_params("parallel", "parallel"),
    )(h, g, wq, k, v, qg, wo)


FFN_OUT_CHUNK = 512


def _ffn_body(xb_ref, wg_ref, wu_ref, wd_ref, acc_ref):
    xb = xb_ref[...]
    a = (_silu(_dot(xb, wg_ref[...])) * _dot(xb, wu_ref[...])).astype(BF16)
    for c in range(0, acc_ref.shape[-1], FFN_OUT_CHUNK):
        acc_ref[:, c:c + FFN_OUT_CHUNK] += _dot(a, wd_ref[:, c:c + FFN_OUT_CHUNK])


def _dense_ffn_kernel(h_ref, g_ref, wg_ref, wu_ref, wd_ref, o_ref, xb_ref):
    @pl.when(pl.program_id(1) == 0)
    def _():
        xb_ref[...] = _rms(h_ref[...], g_ref[...]).astype(BF16)
        o_ref[...] = h_ref[...]

    _ffn_body(xb_ref, wg_ref, wu_ref, wd_ref, o_ref)


def dense_ffn(h, g, wg, wu, wd, tm, tf):
    t, d = h.shape
    f = wg.shape[1]
    row = pl.BlockSpec((tm, d), lambda i, j: (i, 0))
    return pl.pallas_call(
        _dense_ffn_kernel,
        grid=(t // tm, f // tf),
        in_specs=[row, pl.BlockSpec((1, d), lambda i, j: (0, 0)),
                  pl.BlockSpec((d, tf), lambda i, j: (0, j)), pl.BlockSpec((d, tf), lambda i, j: (0, j)),
                  pl.BlockSpec((tf, d), lambda i, j: (j, 0))],
        out_specs=pl.BlockSpec((tm, d), lambda i, j: (i, 0), pipeline_mode=pl.Buffered(1)),
        out_shape=jax.ShapeDtypeStruct((t, d), F32),
        scratch_shapes=[pltpu.VMEM((tm, d), BF16)],
        compiler_params=_params("parallel", "arbitrary"),
    )(h, g, wg, wu, wd)


def _expert_ffn_kernel(te_ref, nu_ref, x_ref, wg_ref, wu_ref, wd_ref, o_ref, xb_ref):
    del te_ref

    @pl.when(pl.program_id(1) == 0)
    def _():
        xb_ref[...] = x_ref[...].astype(BF16)
        o_ref[...] = jnp.zeros_like(o_ref)

    @pl.when(pl.program_id(0) < nu_ref[0])
    def _():
        _ffn_body(xb_ref, wg_ref, wu_ref, wd_ref, o_ref)


def expert_ffn(xs, tile_expert, n_used, wg, wu, wd, tm, tf):
    r, d = xs.shape
    f = wg.shape[-1]
    nj = f // tf
    ti = lambda i, nu: jnp.minimum(i, nu[0] - 1)
    tj = lambda i, j, nu: jnp.where(i < nu[0], j, nj - 1)
    row = pl.BlockSpec((tm, d), lambda i, j, te, nu: (ti(i, nu), 0))
    w_in = pl.BlockSpec((None, d, tf), lambda i, j, te, nu: (te[ti(i, nu)], 0, tj(i, j, nu)))
    w_dn = pl.BlockSpec((None, tf, d), lambda i, j, te, nu: (te[ti(i, nu)], tj(i, j, nu), 0))
    return pl.pallas_call(
        _expert_ffn_kernel,
        grid_spec=pltpu.PrefetchScalarGridSpec(
            num_scalar_prefetch=2,
            grid=(r // tm, nj),
            in_specs=[row, w_in, w_in, w_dn],
            out_specs=pl.BlockSpec((tm, d), lambda i, j, te, nu: (i, 0), pipeline_mode=pl.Buffered(1)),
            scratch_shapes=[pltpu.VMEM((tm, d), BF16)],
        ),
        out_shape=jax.ShapeDtypeStruct((r, d), F32),
        compiler_params=_params("arbitrary", "arbitrary"),
    )(tile_expert, n_used, xs, wg, wu, wd)


def _router_kernel(h_ref, g_ref, rw_ref, rb_ref, tri_ref, hn_ref, info_ref, cnt_ref, carry_ref):
    @pl.when(pl.program_id(0) == 0)
    def _():
        carry_ref[...] = jnp.zeros_like(carry_ref)

    hn = _rms(h_ref[...], g_ref[...])
    hn_ref[...] = hn
    tm = hn.shape[0]
    x3 = _split3(hn)
    w3 = _split3(rw_ref[...])
    logits = rb_ref[...]
    for a, bpart in ((1, 0), (0, 1), (0, 0)):
        logits = logits + _dot(x3[a], w3[bpart])
    lane = lax.broadcasted_iota(jnp.int32, (tm, LANES), 1).astype(F32)
    logits = jnp.where(lane < N_EXPERTS, logits, NEG_BIG)
    v_a = jnp.max(logits, axis=-1, keepdims=True)
    e_a = jnp.min(jnp.where(logits == v_a, lane, float(LANES)), axis=-1, keepdims=True)
    rest = jnp.where(lane == e_a, NEG_BIG, logits)
    v_b = jnp.max(rest, axis=-1, keepdims=True)
    e_b = jnp.min(jnp.where(rest == v_b, lane, float(LANES)), axis=-1, keepdims=True)
    t = jnp.exp(v_b - v_a)
    w_a = 1.0 / (1.0 + t)
    w_b = t * w_a
    chosen = (lane == e_a) | (lane == e_b)
    onehot = jnp.where(chosen, 1.0, 0.0)
    before = _dot(tri_ref[...], onehot.astype(BF16)) + carry_ref[...]
    carry_ref[...] += jnp.sum(onehot, axis=0, keepdims=True)
    cnt_ref[...] = carry_ref[...]
    rank_a = jnp.sum(jnp.where(lane == e_a, before, 0.0), axis=-1, keepdims=True)
    rank_b = jnp.sum(jnp.where(lane == e_b, before, 0.0), axis=-1, keepdims=True)
    info_ref[...] = jnp.where(lane == 0, e_a, jnp.where(lane == 1, e_b, jnp.where(
        lane == 2, w_a, jnp.where(lane == 3, w_b, jnp.where(lane == 4, rank_a, jnp.where(lane == 5, rank_b, 0.0))))))


def router(h, g, rw_pad, rb_pad, tm):
    t, d = h.shape
    tri = jnp.tril(jnp.ones((tm, tm), F32), -1).astype(BF16)
    row = lambda w: pl.BlockSpec((tm, w), lambda i: (i, 0))
    const = lambda shape: pl.BlockSpec(shape, lambda i: (0, 0))
    return pl.pallas_call(
        _router_kernel,
        grid=(t // tm,),
        in_specs=[row(d), const((1, d)), const((d, LANES)), const((1, LANES)), const((tm, tm))],
        out_specs=[row(d), row(LANES), const((1, LANES))],
        out_shape=[jax.ShapeDtypeStruct((t, d), F32), jax.ShapeDtypeStruct((t, LANES), F32),
                   jax.ShapeDtypeStruct((1, LANES), F32)],
        scratch_shapes=[pltpu.VMEM((1, LANES), F32)],
        compiler_params=_params("arbitrary"),
    )(h, g, rw_pad, rb_pad, tri)


def _row_copy(src_ref, src_row, dst_ref, dst_row, sem):
    return pltpu.make_async_copy(src_ref.at[pl.ds(src_row, 1), :], dst_ref.at[pl.ds(dst_row, 1), :], sem)


def _dispatch_kernel(da_ref, db_ref, gs_ref, gl_ref, x_ref, xs_ref, zero_ref, sem):
    tm = x_ref.shape[0]
    base = pl.program_id(0) * tm

    @pl.when(pl.program_id(0) == 0)
    def _():
        zero_ref[...] = jnp.zeros_like(zero_ref)
        for g in range(gs_ref.shape[0]):
            def fill(r, _, g=g):
                _row_copy(zero_ref, 0, xs_ref, gs_ref[g] + r, sem).start()
                return 0

            def filled(r, _):
                _row_copy(zero_ref, 0, xs_ref, 0, sem).wait()
                return 0

            lax.fori_loop(0, gl_ref[g], fill, 0)
            lax.fori_loop(0, gl_ref[g], filled, 0)

    def issue(r, _):
        _row_copy(x_ref, r, xs_ref, da_ref[base + r], sem).start()
        _row_copy(x_ref, r, xs_ref, db_ref[base + r], sem).start()
        return 0

    def drain(r, _):
        _row_copy(x_ref, 0, xs_ref, 0, sem).wait()
        _row_copy(x_ref, 0, xs_ref, 0, sem).wait()
        return 0

    lax.fori_loop(0, tm, issue, 0, unroll=8)
    lax.fori_loop(0, tm, drain, 0, unroll=8)


def dispatch(x, dest_a, dest_b, gap_start, gap_len, n_rows, tm):
    t, d = x.shape
    return pl.pallas_call(
        _dispatch_kernel,
        grid_spec=pltpu.PrefetchScalarGridSpec(
            num_scalar_prefetch=4,
            grid=(t // tm,),
            in_specs=[pl.BlockSpec((tm, d), lambda i, *_: (i, 0))],
            out_specs=pl.BlockSpec(memory_space=pl.ANY),
            scratch_shapes=[pltpu.VMEM((SUBLANES, d), x.dtype), pltpu.SemaphoreType.DMA(())],
        ),
        out_shape=jax.ShapeDtypeStruct((n_rows, d), x.dtype),
        compiler_params=_params("arbitrary"),
    )(dest_a, dest_b, gap_start, gap_len, x)


def _combine_kernel(da_ref, db_ref, h_ref, info_ref, ys_ref, o_ref, ya_ref, yb_ref, sem):
    tm = h_ref.shape[0]
    base = pl.program_id(0) * tm

    def issue(r, _):
        _row_copy(ys_ref, da_ref[base + r], ya_ref, r, sem).start()
        _row_copy(ys_ref, db_ref[base + r], yb_ref, r, sem).start()
        return 0

    def drain(r, _):
        _row_copy(ys_ref, 0, ya_ref, 0, sem).wait()
        _row_copy(ys_ref, 0, yb_ref, 0, sem).wait()
        return 0

    lax.fori_loop(0, tm, issue, 0, unroll=8)
    lax.fori_loop(0, tm, drain, 0, unroll=8)
    info = info_ref[...]
    o_ref[...] = h_ref[...] + info[:, 2:3] * ya_ref[...] + info[:, 3:4] * yb_ref[...]


def combine(h, info, ys, dest_a, dest_b, tm):
    t, d = h.shape
    row = lambda w: pl.BlockSpec((tm, w), lambda i, da, db: (i, 0))
    return pl.pallas_call(
        _combine_kernel,
        grid_spec=pltpu.PrefetchScalarGridSpec(
            num_scalar_prefetch=2,
            grid=(t // tm,),
            in_specs=[row(d), row(LANES), pl.BlockSpec(memory_space=pl.ANY)],
            out_specs=row(d),
            scratch_shapes=[pltpu.VMEM((tm, d), F32), pltpu.VMEM((tm, d), F32), pltpu.SemaphoreType.DMA(())],
        ),
        out_shape=jax.ShapeDtypeStruct((t, d), F32),
        compiler_params=_params("arbitrary"),
    )(dest_a, dest_b, h, info, ys)


def moe_ffn(h, g, rw, rb, wg, wu, wd, tm_route, tm_rows, tf, tm_move):
    t, d = h.shape
    rw_pad = jnp.zeros((d, LANES), F32).at[:, :N_EXPERTS].set(rw)
    rb_pad = jnp.zeros((1, LANES), F32).at[0, :N_EXPERTS].set(rb)
    hn, info, counts = router(h, g, rw_pad, rb_pad, tm_route)
    counts = counts[0, :N_EXPERTS].astype(jnp.int32)
    padded = (counts + tm_rows - 1) // tm_rows * tm_rows
    ends = jnp.cumsum(padded)
    starts = ends - padded
    e_a, e_b = info[:, 0].astype(jnp.int32), info[:, 1].astype(jnp.int32)
    dest_a = starts[e_a] + info[:, 4].astype(jnp.int32)
    dest_b = starts[e_b] + info[:, 5].astype(jnp.int32)
    n_rows = 2 * t + N_EXPERTS * tm_rows
    n_tiles = n_rows // tm_rows
    n_used = (ends[-1] // tm_rows).astype(jnp.int32).reshape(1)
    tile_start = jnp.arange(n_tiles, dtype=jnp.int32) * tm_rows
    tile_expert = jnp.minimum(jnp.sum(tile_start[:, None] >= ends[None, :], axis=-1), N_EXPERTS - 1).astype(jnp.int32)
    gap_start = jnp.concatenate([starts + counts, ends[-1:]]).astype(jnp.int32)
    gap_len = jnp.concatenate([padded - counts, n_rows - ends[-1:]]).astype(jnp.int32)
    xs = dispatch(hn, dest_a, dest_b, gap_start, gap_len, n_rows, tm_move)
    ys = expert_ffn(xs, tile_expert, n_used, wg, wu, wd, tm_rows, tf)
    return combine(h, info, ys, dest_a, dest_b, tm_move)


def _pack_w_in(w):
    d = w.shape[0]
    pts = [0, 512, 1024, 1536, 1544, 2056, 2568, 3080, 3592, 4616, 4632, 5656]
    fq, fk, fv, ff, ca, cg, gq, gk, gv, glr, gr = [w[:, a:b] for a, b in zip(pts[:-1], pts[1:])]
    pad = jnp.zeros((d, LANES - FOX_HEADS - GLA_GATE_RANK), w.dtype)
    return jnp.concatenate([fq, fk, fv, ca, cg, gq, gk, gv, gr, ff, glr, pad], axis=1).astype(BF16)


def _tile(n, pref):
    return min(n, pref)


def kernel(x, mem, mem_norm, mix_norm, w_in, fox_forget_bias, fox_q_norm, fox_k_norm, conv_w, conv_b,
           conv_ln_g, conv_ln_b, gla_gate_w2, gla_gate_b, gla_out_norm, w_out, xa_norm, xa_wq, xa_wk, xa_wv,
           xa_q_norm, xa_k_norm, xa_wo, ffn_norm, dense_wg, dense_wu, dense_wd, router_w, router_b,
           moe_wg, moe_wu, moe_wd):
    bsz, s, d = x.shape
    t = bsz * s
    depth = mix_norm.shape[0]
    ts = _tile(s, 512)
    row = lambda v: v.reshape(1, -1).astype(F32)

    k_mem, v_mem = mem_kv(mem, row(mem_norm), xa_wk.astype(BF16), xa_wv.astype(BF16),
                          xa_k_norm.reshape(depth, 1, XA_HEAD_DIM))
    h = x.reshape(t, d)
    for l in range(depth):
        proj = norm_matmul(h, row(mix_norm[l]), _pack_w_in(w_in[l]), _tile(t, 1024), PACK_W // 3)
        proj = proj.reshape(bsz, s, PACK_W)
        bf_pad = jnp.zeros((1, LANES), F32).at[0, :FOX_HEADS].set(fox_forget_bias[l])
        qg, kg = row(fox_q_norm[l]), row(fox_k_norm[l])
        fox = fox_attention(proj, bf_pad, qg, kg, ts)
        conv_w_pad = jnp.zeros((CONV_HALO, CONV_W), F32).at[:CONV_TAPS].set(conv_w[l])
        conv = conformer_conv(proj, conv_w_pad, row(conv_b[l]), row(conv_ln_g[l]), row(conv_ln_b[l]), ts)
        w2_pad = jnp.zeros((LANES, GLA_KW), F32).at[FOX_HEADS:FOX_HEADS + GLA_GATE_RANK].set(gla_gate_w2[l])
        gla_o = gla(proj, w2_pad.astype(BF16), row(gla_gate_b[l]), row(gla_out_norm[l]), ts)
        h = out_proj(h, fox, conv.reshape(t, CONV_W), gla_o.reshape(t, GLA_VW), w_out[l].astype(BF16),
                     _tile(t, 512))
        h = cross_attention(h.reshape(bsz, s, d), row(xa_norm[l]), xa_wq[l].astype(BF16), k_mem[l], v_mem[l],
                            row(xa_q_norm[l]), xa_wo[l].astype(BF16), ts).reshape(t, d)
        j = l // 2
        if l % 2 == 0:
            h = dense_ffn(h, row(ffn_norm[l]), dense_wg[j].astype(BF16), dense_wu[j].astype(BF16),
                          dense_wd[j].astype(BF16), _tile(t, 1024), 512)
        else:
            h = moe_ffn(h, row(ffn_norm[l]), router_w[j], router_b[j], moe_wg[j].astype(BF16),
                        moe_wu[j].astype(BF16), moe_wd[j].astype(BF16), _tile(t, 512), _tile(t, 512), 1024,
                        _tile(t, 256))
    return h.reshape(bsz, s, d)
```
